```python
import math
import jax, jax.numpy as jnp
from jax import lax
import numpy as np

D_MODEL = 4096
BATCH = 4
SEQ = 2048
DEPTH = 2
DEC_BATCH = 16
DEC_SEQ = 64
PAST_LEN = 1024

CHUNK = 64
N_MIXERS = 2
N_ATTN_LAYERS = (DEPTH + 1) // 2
N_CONV_LAYERS = DEPTH // 2
N_HEADS = 16
HEAD_DIM = D_MODEL // (2 * N_HEADS)
V_DIM = 2 * HEAD_DIM
Q_BLOCK = 128
CONV_WIDTH = 31
PEER_HEADS = 8
N_KEYS = 128
N_EXPERTS = N_KEYS * N_KEYS
PEER_TOPK = 16
PEER_QDIM = 256
PEER_HALF = PEER_QDIM // 2
PEER_TOKEN_BLOCK = 128
EPS = 1e-6

kernel_name = "diffattn_conformer_peer_stream_step"


def _rmsnorm(x, g):
    xf = x.astype(jnp.float32)
    y = xf * lax.rsqrt(jnp.mean(xf * xf, axis=-1, keepdims=True) + EPS)
    return y.astype(x.dtype) * g


def _layernorm(x, g, b):
    xf = x.astype(jnp.float32)
    mu = jnp.mean(xf, axis=-1, keepdims=True)
    var = jnp.mean(jnp.square(xf - mu), axis=-1, keepdims=True)
    return ((xf - mu) * lax.rsqrt(var + EPS)).astype(x.dtype) * g + b


def _diff_attend(q, k, v, lam, mask):
    s = jnp.einsum('bqhcd,bkhcd->bhcqk', q, k).astype(jnp.float32) * (HEAD_DIM ** -0.5)
    if mask is not None:
        s = jnp.where(mask, s, -jnp.inf)
    p = jax.nn.softmax(s, axis=-1)
    a = p[:, :, 0] - lam * p[:, :, 1]
    return jnp.einsum('bhqk,bkhe->bqhe', a.astype(v.dtype), v)


def _attn_prompt(q, k, v, lam):
    B, S = q.shape[0], q.shape[1]
    nblk = S // Q_BLOCK
    qb = q.reshape(B, nblk, Q_BLOCK, N_HEADS, 2, HEAD_DIM).transpose(1, 0, 2, 3, 4, 5)
    kchunk = jnp.arange(S) // CHUNK

    def blk(args):
        qi, bi = args
        qchunk = (bi * Q_BLOCK + jnp.arange(Q_BLOCK)) // CHUNK
        mask = kchunk[None, :] <= qchunk[:, None]
        return _diff_attend(qi, k, v, lam, mask)

    o = lax.map(blk, (qb, jnp.arange(nblk)))
    return o.transpose(1, 0, 2, 3, 4).reshape(B, S, N_HEADS, V_DIM)


def _diff_attn_mixer(h, k_past, v_past, w_qkv, lq1, lk1, lq2, lk2, subln_g, w_o, lam_init):
    B, T, _ = h.shape
    q, k, v = jnp.split(h @ w_qkv, 3, axis=-1)
    q = q.reshape(B, T, N_HEADS, 2, HEAD_DIM)
    k = k.reshape(B, T, N_HEADS, 2, HEAD_DIM)
    v = v.reshape(B, T, N_HEADS, V_DIM)
    f32 = jnp.float32
    lam = (jnp.exp(jnp.sum(lq1.astype(f32) * lk1.astype(f32)))
           - jnp.exp(jnp.sum(lq2.astype(f32) * lk2.astype(f32))) + lam_init)
    if k_past is None:
        o = _attn_prompt(q, k, v, lam)
    else:
        kk = jnp.concatenate([k_past, k], axis=1)
        vv = jnp.concatenate([v_past, v], axis=1)
        o = _diff_attend(q, kk, vv, lam, None)
    o = _rmsnorm(o, subln_g) * (1.0 - lam_init)
    return o.reshape(B, T, D_MODEL) @ w_o, k, v


def _conv_mixer(h, conv_past, w_pw1, b_pw1, w_dw, b_dw, ln_g, ln_b, w_pw2, b_pw2):
    B, T, _ = h.shape
    a, gate = jnp.split(h @ w_pw1 + b_pw1, 2, axis=-1)
    g = a * jax.nn.sigmoid(gate)
    if conv_past is None:
        conv_past = jnp.zeros((B, CONV_WIDTH - 1, D_MODEL), g.dtype)
    gp = jnp.concatenate([conv_past, g], axis=1)
    c = lax.conv_general_dilated(gp, w_dw[:, None, :], (1,), 'VALID',
                                 dimension_numbers=('NWC', 'WIO', 'NWC'),
                                 feature_group_count=D_MODEL) + b_dw
    z = jax.nn.silu(_layernorm(c, ln_g, ln_b)) @ w_pw2 + b_pw2
    return z, gp[:, -(CONV_WIDTH - 1):]


def _peer(h, w_pq, sub_keys, u_tab, v_tab):
    B, T, D = h.shape
    n = B * T
    x = jnp.pad(h.reshape(n, D), ((0, (-n) % PEER_TOKEN_BLOCK), (0, 0)))
    nt = x.shape[0]
    q = (x @ w_pq).reshape(nt, PEER_HEADS, 2, PEER_HALF)
    s = jnp.einsum('nhcd,hckd->nhck', q, sub_keys).astype(jnp.float32)
    s_top, i_top = lax.top_k(s, PEER_TOPK)
    cand = (s_top[:, :, 0, :, None] + s_top[:, :, 1, None, :]).reshape(nt, PEER_HEADS, -1)
    cand_idx = (i_top[:, :, 0, :, None] * N_KEYS + i_top[:, :, 1, None, :]).reshape(nt, PEER_HEADS, -1)
    best, pos = lax.top_k(cand, PEER_TOPK)
    idx = jnp.take_along_axis(cand_idx, pos, axis=-1)
    gate = jax.nn.softmax(best, axis=-1).astype(x.dtype)
    nb = nt // PEER_TOKEN_BLOCK
    n_sel = PEER_HEADS * PEER_TOPK

    def blk(args):
        xb, ib, gb = args
        u = jnp.take(u_tab, ib, axis=0)
        v = jnp.take(v_tab, ib, axis=0)
        act = jax.nn.gelu(jnp.einsum('td,ted->te', xb, u), approximate=False)
        return jnp.einsum('te,ted->td', act * gb, v)

    y = lax.map(blk, (x.reshape(nb, PEER_TOKEN_BLOCK, D),
                      idx.reshape(nb, PEER_TOKEN_BLOCK, n_sel),
                      gate.reshape(nb, PEER_TOKEN_BLOCK, n_sel)))
    return y.reshape(nt, D)[:n].reshape(B, T, D)


def setup_inputs(seed: int = 0) -> dict:
    key = jax.random.key(seed)
    ks = jax.random.split(key, 32)
    nrm = jax.random.normal
    D = D_MODEL
    return {
        "x_prompt": nrm(ks[0], (BATCH, SEQ, D), jnp.float32),
        "x_sample": nrm(ks[1], (DEC_BATCH, DEC_SEQ, D), jnp.float32),
        "cache_k": nrm(ks[2], (N_ATTN_LAYERS, DEC_BATCH, PAST_LEN, N_HEADS, 2, HEAD_DIM), jnp.float32),
        "cache_v": nrm(ks[3], (N_ATTN_LAYERS, DEC_BATCH, PAST_LEN, N_HEADS, V_DIM), jnp.float32),
        "state_conv": 0.5 * nrm(ks[4], (N_CONV_LAYERS, DEC_BATCH, CONV_WIDTH - 1, D), jnp.float32),
        "mixer_norm_g": 1.0 + 0.01 * nrm(ks[5], (DEPTH, D), jnp.float32),
        "ffn_norm_g": 1.0 + 0.01 * nrm(ks[6], (DEPTH, D), jnp.float32),
        "final_norm_g": 1.0 + 0.01 * nrm(ks[7], (D,), jnp.float32),
        "w_qkv": nrm(ks[8], (N_ATTN_LAYERS, D, 3 * D), jnp.float32) * D ** -0.5,
        "lambda_q1": 0.1 * nrm(ks[9], (N_ATTN_LAYERS, HEAD_DIM), jnp.float32),
        "lambda_k1": 0.1 * nrm(ks[10], (N_ATTN_LAYERS, HEAD_DIM), jnp.float32),
        "lambda_q2": 0.1 * nrm(ks[11], (N_ATTN_LAYERS, HEAD_DIM), jnp.float32),
        "lambda_k2": 0.1 * nrm(ks[12], (N_ATTN_LAYERS, HEAD_DIM), jnp.float32),
        "subln_g": 1.0 + 0.01 * nrm(ks[13], (N_ATTN_LAYERS, V_DIM), jnp.float32),
        "w_o": nrm(ks[14], (N_ATTN_LAYERS, D, D), jnp.float32) * D ** -0.5,
        "w_pw1": nrm(ks[15], (N_CONV_LAYERS, D, 2 * D), jnp.float32) * D ** -0.5,
        "b_pw1": 0.01 * nrm(ks[16], (N_CONV_LAYERS, 2 * D), jnp.float32),
        "w_dw": nrm(ks[17], (N_CONV_LAYERS, CONV_WIDTH, D), jnp.float32) * CONV_WIDTH ** -0.5,
        "b_dw": 0.01 * nrm(ks[18], (N_CONV_LAYERS, D), jnp.float32),
        "conv_ln_g": 1.0 + 0.01 * nrm(ks[19], (N_CONV_LAYERS, D), jnp.float32),
        "conv_ln_b": 0.01 * nrm(ks[20], (N_CONV_LAYERS, D), jnp.float32),
        "w_pw2": nrm(ks[21], (N_CONV_LAYERS, D, D), jnp.float32) * D ** -0.5,
        "b_pw2": 0.01 * nrm(ks[22], (N_CONV_LAYERS, D), jnp.float32),
        "peer_wq": nrm(ks[23], (DEPTH, D, PEER_HEADS * PEER_QDIM), jnp.float32) * D ** -0.5,
        "peer_sub_keys": nrm(ks[24], (DEPTH, PEER_HEADS, 2, N_KEYS, PEER_HALF), jnp.float32) * PEER_HALF ** -0.5,
        "peer_u": nrm(ks[25], (DEPTH, N_EXPERTS, D), jnp.float32) * D ** -0.5,
        "peer_v": nrm(ks[26], (DEPTH, N_EXPERTS, D), jnp.float32) * PEER_HEADS ** -0.5,
    }


def reference(x_prompt, x_sample, cache_k, cache_v, state_conv, mixer_norm_g, ffn_norm_g, final_norm_g,
              w_qkv, lambda_q1, lambda_k1, lambda_q2, lambda_k2, subln_g, w_o,
              w_pw1, b_pw1, w_dw, b_dw, conv_ln_g, conv_ln_b, w_pw2, b_pw2,
              peer_wq, peer_sub_keys, peer_u, peer_v):
    hp, hs = x_prompt, x_sample
    kp_l, vp_l, ks_l, vs_l, cp_l, cs_l = [], [], [], [], [], []
    for i in range(DEPTH):
        if i % N_MIXERS == 0:
            a = i // N_MIXERS
            lam_init = 0.8 - 0.6 * math.exp(-0.3 * i)
            w = (w_qkv[a], lambda_q1[a], lambda_k1[a], lambda_q2[a], lambda_k2[a], subln_g[a], w_o[a], lam_init)
            yp, kp, vp = _diff_attn_mixer(_rmsnorm(hp, mixer_norm_g[i]), None, None, *w)
            ys, kn, vn = _diff_attn_mixer(_rmsnorm(hs, mixer_norm_g[i]), cache_k[a], cache_v[a], *w)
            kp_l.append(kp); vp_l.append(vp); ks_l.append(kn); vs_l.append(vn)
        else:
            c = i // N_MIXERS
            w = (w_pw1[c], b_pw1[c], w_dw[c], b_dw[c], conv_ln_g[c], conv_ln_b[c], w_pw2[c], b_pw2[c])
            yp, cp = _conv_mixer(_rmsnorm(hp, mixer_norm_g[i]), None, *w)
            ys, cn = _conv_mixer(_rmsnorm(hs, mixer_norm_g[i]), state_conv[c], *w)
            cp_l.append(cp); cs_l.append(cn)
        hp = hp + yp
        hs = hs + ys
        pw = (peer_wq[i], peer_sub_keys[i], peer_u[i], peer_v[i])
        hp = hp + _peer(_rmsnorm(hp, ffn_norm_g[i]), *pw)
        hs = hs + _peer(_rmsnorm(hs, ffn_norm_g[i]), *pw)
    y_prompt = _rmsnorm(hp, final_norm_g)
    y_sample = _rmsnorm(hs, final_norm_g)
    new_k_prompt = jnp.stack(kp_l)
    new_v_prompt = jnp.stack(vp_l)
    new_conv_prompt = jnp.stack(cp_l)
    new_k_sample = jnp.stack(ks_l)
    new_v_sample = jnp.stack(vs_l)
    new_conv_sample = jnp.stack(cs_l)
    return (y_prompt, y_sample, new_k_prompt, new_v_prompt, new_conv_prompt, new_k_sample, new_v_sample, new_conv_sample)
```

```python
import functools
import math

import jax
import jax.numpy as jnp
from jax import lax
from jax.experimental import pallas as pl
from jax.experimental.pallas import tpu as pltpu

EPS = 1e-6
CHUNK = 64
CONV_WIDTH = 31
CONV_HALO = 32
PEER_TOPK = 16
LANES = 128
VMEM_LIMIT = 48 * 1024 * 1024

_NT = (((1,), (1,)), ((), ()))
_F32 = jnp.float32
_BF16 = jnp.bfloat16


def _params(*sem):
    return pltpu.CompilerParams(dimension_semantics=sem, vmem_limit_bytes=VMEM_LIMIT)


def _rmsnorm_kernel(x_ref, g_ref, o_ref):
    x = x_ref[...]
    ms = jnp.mean(x * x, axis=-1, keepdims=True)
    o_ref[...] = (x * lax.rsqrt(ms + EPS) * g_ref[...]).astype(o_ref.dtype)


def _rmsnorm(x, g, out_dtype, tr=256):
    m, d = x.shape
    return pl.pallas_call(
        _rmsnorm_kernel,
        grid=(m // tr,),
        in_specs=[pl.BlockSpec((tr, d), lambda i: (i, 0)), pl.BlockSpec((1, d), lambda i: (0, 0))],
        out_specs=pl.BlockSpec((tr, d), lambda i: (i, 0)),
        out_shape=jax.ShapeDtypeStruct((m, d), out_dtype),
        compiler_params=_params("parallel"),
        name="rmsnorm",
    )(x, g.reshape(1, d))


def _mm_kernel(*refs, has_bias, has_resid, n_out):
    x_ref, w_ref = refs[0], refs[1]
    pos = 2
    b_ref = r_ref = None
    if has_bias:
        b_ref = refs[pos]
        pos += 1
    if has_resid:
        r_ref = refs[pos]
        pos += 1
    out_refs = refs[pos:pos + n_out]
    acc_ref = refs[pos + n_out]
    k = pl.program_id(2)

    @pl.when(k == 0)
    def _():
        acc_ref[...] = jnp.zeros_like(acc_ref)

    acc_ref[...] += jnp.dot(x_ref[...], w_ref[...], preferred_element_type=_F32)

    @pl.when(k == pl.num_programs(2) - 1)
    def _():
        r = acc_ref[...]
        if has_bias:
            r = r + b_ref[...]
        if has_resid:
            r = r + r_ref[...]
        for o in out_refs:
            o[...] = r.astype(o.dtype)


def _mm_tiles(m, n, k):
    return min(1024, m), min(1024, n), min(512, k)


def _matmul(x, w, *, bias=None, resid=None, out_dtypes=(_F32,), name="matmul"):
    m, kdim = x.shape
    n = w.shape[1]
    tm, tn, tk = _mm_tiles(m, n, kdim)
    in_specs = [pl.BlockSpec((tm, tk), lambda i, j, k: (i, k)),
                pl.BlockSpec((tk, tn), lambda i, j, k: (k, j))]
    args = [x, w]
    if bias is not None:
        in_specs.append(pl.BlockSpec((1, tn), lambda i, j, k: (0, j)))
        args.append(bias.reshape(1, n))
    if resid is not None:
        in_specs.append(pl.BlockSpec((tm, tn), lambda i, j, k: (i, j)))
        args.append(resid)
    outs = pl.pallas_call(
        functools.partial(_mm_kernel, has_bias=bias is not None, has_resid=resid is not None,
                          n_out=len(out_dtypes)),
        grid=(m // tm, n // tn, kdim // tk),
        in_specs=in_specs,
        out_specs=[pl.BlockSpec((tm, tn), lambda i, j, k: (i, j)) for _ in out_dtypes],
        out_shape=[jax.ShapeDtypeStruct((m, n), dt) for dt in out_dtypes],
        scratch_shapes=[pltpu.VMEM((tm, tn), _F32)],
        compiler_params=_params("parallel", "parallel", "arbitrary"),
        name=name,
    )(*args)
    return outs if len(out_dtypes) > 1 else outs[0]


def _glu_kernel(x_ref, wa_ref, wg_ref, ba_ref, bg_ref, o_ref, acc_a, acc_g):
    k = pl.program_id(2)

    @pl.when(k == 0)
    def _():
        acc_a[...] = jnp.zeros_like(acc_a)
        acc_g[...] = jnp.zeros_like(acc_g)

    x = x_ref[...]
    acc_a[...] += jnp.dot(x, wa_ref[...], preferred_element_type=_F32)
    acc_g[...] += jnp.dot(x, wg_ref[...], preferred_element_type=_F32)

    @pl.when(k == pl.num_programs(2) - 1)
    def _():
        a = acc_a[...] + ba_ref[...]
        gate = acc_g[...] + bg_ref[...]
        o_ref[...] = a * jax.nn.sigmoid(gate)


def _glu_matmul(x, w, b):
    m, kdim = x.shape
    n = w.shape[1] // 2
    tm, tn, tk = _mm_tiles(m, n, kdim)
    nj = n // tn
    b2 = b.reshape(1, 2 * n)
    return pl.pallas_call(
        _glu_kernel,
        grid=(m // tm, nj, kdim // tk),
        in_specs=[pl.BlockSpec((tm, tk), lambda i, j, k: (i, k)),
                  pl.BlockSpec((tk, tn), lambda i, j, k: (k, j)),
                  pl.BlockSpec((tk, tn), lambda i, j, k: (k, j + nj)),
                  pl.BlockSpec((1, tn), lambda i, j, k: (0, j)),
                  pl.BlockSpec((1, tn), lambda i, j, k: (0, j + nj))],
        out_specs=pl.BlockSpec((tm, tn), lambda i, j, k: (i, j)),
        out_shape=jax.ShapeDtypeStruct((m, n), _F32),
        scratch_shapes=[pltpu.VMEM((tm, tn), _F32), pltpu.VMEM((tm, tn), _F32)],
        compiler_params=_params("parallel", "parallel", "arbitrary"),
        name="glu_matmul",
    )(x, w, w, b2, b2)


def _lambda(lq1_ref, lk1_ref, lq2_ref, lk2_ref, lam_init):
    s1 = jnp.sum(lq1_ref[...] * lk1_ref[...], axis=-1, keepdims=True)
    s2 = jnp.sum(lq2_ref[...] * lk2_ref[...], axis=-1, keepdims=True)
    return jnp.exp(s1) - jnp.exp(s2) + lam_init


def _sub_layernorm(o, g_ref, lam_init):
    ms = jnp.mean(o * o, axis=-1, keepdims=True)
    return (o * lax.rsqrt(ms + EPS)) * g_ref[...] * (1.0 - lam_init)


def _attn_prompt_kernel(lq1_ref, lk1_ref, lq2_ref, lk2_ref, g_ref, q_ref, k_ref, v_ref, o_ref,
                        m_ref, l_ref, acc_ref, *, tq, dh, lam_init):
    qi = pl.program_id(2)
    scale = dh ** -0.5
    q = q_ref[...]
    m_ref[...] = jnp.full_like(m_ref, -jnp.inf)
    l_ref[...] = jnp.zeros_like(l_ref)
    acc_ref[...] = jnp.zeros_like(acc_ref)

    def tile(kj, mask):
        off = pl.multiple_of(kj * tq, tq)
        k = k_ref[pl.ds(off, tq), :]
        v = v_ref[pl.ds(off, tq), :]
        for c in range(2):
            s = lax.dot_general(q[:, c * dh:(c + 1) * dh], k[:, c * dh:(c + 1) * dh], _NT,
                                preferred_element_type=_F32) * scale
            if mask is not None:
                s = jnp.where(mask, s, -jnp.inf)
            m_old = m_ref[c]
            m_new = jnp.maximum(m_old, jnp.max(s, axis=-1, keepdims=True))
            alpha = jnp.exp(m_old - m_new)
            p = jnp.exp(s - m_new)
            l_ref[c] = alpha * l_ref[c] + jnp.sum(p, axis=-1, keepdims=True)
            acc_ref[c] = alpha * acc_ref[c] + jnp.dot(p.astype(_BF16), v, preferred_element_type=_F32)
            m_ref[c] = m_new

    def body(kj, carry):
        tile(kj, None)
        return carry

    lax.fori_loop(0, qi, body, 0)
    rows = lax.broadcasted_iota(jnp.int32, (tq, tq), 0) // CHUNK
    cols = lax.broadcasted_iota(jnp.int32, (tq, tq), 1) // CHUNK
    tile(qi, cols <= rows)

    lam = _lambda(lq1_ref, lk1_ref, lq2_ref, lk2_ref, lam_init)
    o = acc_ref[0] / l_ref[0] - lam * (acc_ref[1] / l_ref[1])
    o_ref[...] = _sub_layernorm(o, g_ref, lam_init).astype(o_ref.dtype)


def _attn_prompt(q, k, v, lam_params, subln_g, *, batch, seq, n_heads, dh, lam_init, tq=256):
    hw = 2 * dh
    nq = seq // tq
    small = pl.BlockSpec((1, dh), lambda b, h, i: (0, 0))
    return pl.pallas_call(
        functools.partial(_attn_prompt_kernel, tq=tq, dh=dh, lam_init=lam_init),
        grid=(batch, n_heads, nq),
        in_specs=[small, small, small, small,
                  pl.BlockSpec((1, hw), lambda b, h, i: (0, 0)),
                  pl.BlockSpec((tq, hw), lambda b, h, i: (b * nq + i, h)),
                  pl.BlockSpec((seq, hw), lambda b, h, i: (b, h)),
                  pl.BlockSpec((seq, hw), lambda b, h, i: (b, h))],
        out_specs=pl.BlockSpec((tq, hw), lambda b, h, i: (b * nq + i, h)),
        out_shape=jax.ShapeDtypeStruct(q.shape, _BF16),
        scratch_shapes=[pltpu.VMEM((2, tq, 1), _F32), pltpu.VMEM((2, tq, 1), _F32),
                        pltpu.VMEM((2, tq, hw), _F32)],
        compiler_params=_params("parallel", "parallel", "arbitrary"),
        name="attn_prompt",
    )(*lam_params, subln_g.reshape(1, hw), q, k, v)


def _attn_sample_kernel(lq1_ref, lk1_ref, lq2_ref, lk2_ref, g_ref, q_ref, kn_ref, vn_ref, ck_ref, cv_ref,
                        o_ref, *, dh, lam_init):
    scale = dh ** -0.5
    q = q_ref[...]
    kn = kn_ref[...]
    kp = ck_ref[0].astype(_BF16)
    lam = _lambda(lq1_ref, lk1_ref, lq2_ref, lk2_ref, lam_init)
    a_p = a_n = None
    for c in range(2):
        qc = q[:, c * dh:(c + 1) * dh]
        s_p = lax.dot_general(qc, kp[:, c * dh:(c + 1) * dh], _NT, preferred_element_type=_F32) * scale
        s_n = lax.dot_general(qc, kn[:, c * dh:(c + 1) * dh], _NT, preferred_element_type=_F32) * scale
        m = jnp.maximum(jnp.max(s_p, axis=-1, keepdims=True), jnp.max(s_n, axis=-1, keepdims=True))
        e_p = jnp.exp(s_p - m)
        e_n = jnp.exp(s_n - m)
        denom = jnp.sum(e_p, axis=-1, keepdims=True) + jnp.sum(e_n, axis=-1, keepdims=True)
        p_p = e_p / denom
        p_n = e_n / denom
        if c == 0:
            a_p, a_n = p_p, p_n
        else:
            a_p, a_n = a_p - lam * p_p, a_n - lam * p_n
    o = (jnp.dot(a_p.astype(_BF16), cv_ref[0].astype(_BF16), preferred_element_type=_F32)
         + jnp.dot(a_n.astype(_BF16), vn_ref[...], preferred_element_type=_F32))
    o_ref[...] = _sub_layernorm(o, g_ref, lam_init).astype(o_ref.dtype)


def _attn_sample(q, k, v, cache_k, cache_v, lam_params, subln_g, *, batch, seq, n_heads, dh, lam_init):
    hw = 2 * dh
    past = cache_k.shape[1]
    small = pl.BlockSpec((1, dh), lambda b, h: (0, 0))
    new = pl.BlockSpec((seq, hw), lambda b, h: (b, h))
    old = pl.BlockSpec((1, past, hw), lambda b, h: (b, 0, h))
    return pl.pallas_call(
        functools.partial(_attn_sample_kernel, dh=dh, lam_init=lam_init),
        grid=(batch, n_heads),
        in_specs=[small, small, small, small, pl.BlockSpec((1, hw), lambda b, h: (0, 0)),
                  new, new, new, old, old],
        out_specs=new,
        out_shape=jax.ShapeDtypeStruct(q.shape, _BF16),
        compiler_params=_params("parallel", "parallel"),
        name="attn_sample",
    )(*lam_params, subln_g.reshape(1, hw), q, k, v, cache_k, cache_v)


def _conv_ln_silu(ext_ref, c_ref, w_ref, bdw_ref, g_ref, b_ref, o_ref, *, rows, lane_chunk=512, row_chunk=16):
    d = c_ref.shape[1]
    first = CONV_HALO - (CONV_WIDTH - 1)

    def chunk(ci, carry):
        l0 = pl.multiple_of(ci * lane_chunk, lane_chunk)
        lanes = pl.ds(l0, lane_chunk)
        for r0 in range(0, rows, row_chunk):
            acc = jnp.zeros((row_chunk, lane_chunk), _F32)
            for j in range(CONV_WIDTH):
                acc = acc + w_ref[pl.ds(j, 1), lanes] * ext_ref[pl.ds(first + r0 + j, row_chunk), lanes]
            c_ref[pl.ds(r0, row_chunk), lanes] = acc + bdw_ref[:, lanes]
        return carry

    lax.fori_loop(0, d // lane_chunk, chunk, 0)
    c = c_ref[...]
    mu = jnp.mean(c, axis=-1, keepdims=True)
    cc = c - mu
    var = jnp.mean(cc * cc, axis=-1, keepdims=True)
    y = cc * lax.rsqrt(var + EPS) * g_ref[...] + b_ref[...]
    o_ref[...] = (y * jax.nn.sigmoid(y)).astype(o_ref.dtype)


def _conv_prompt_kernel(x_ref, halo_ref, w_ref, bdw_ref, g_ref, b_ref, o_ref, ext_ref, c_ref, *, rows):
    i = pl.program_id(1)
    halo = halo_ref[...]
    ext_ref[pl.ds(0, CONV_HALO), :] = jnp.where(i == 0, jnp.zeros_like(halo), halo)
    ext_ref[pl.ds(CONV_HALO, rows), :] = x_ref[...]
    _conv_ln_silu(ext_ref, c_ref, w_ref, bdw_ref, g_ref, b_ref, o_ref, rows=rows)


def _conv_sample_kernel(x_ref, st_ref, w_ref, bdw_ref, g_ref, b_ref, o_ref, ext_ref, c_ref, *, rows):
    ctx = CONV_WIDTH - 1
    ext_ref[pl.ds(0, CONV_HALO - ctx), :] = jnp.zeros((CONV_HALO - ctx, ext_ref.shape[1]), _F32)
    ext_ref[pl.ds(CONV_HALO - ctx, ctx), :] = st_ref[0]
    ext_ref[pl.ds(CONV_HALO, rows), :] = x_ref[...]
    _conv_ln_silu(ext_ref, c_ref, w_ref, bdw_ref, g_ref, b_ref, o_ref, rows=rows)


def _conv_module(x, state, w_dw, b_dw, ln_g, ln_b, *, batch, seq):
    d = x.shape[1]
    vec = lambda a: a.reshape(1, d)
    if state is None:
        rows = 128
        nt = seq // rows
        grid = (batch, nt)
        per = rows // CONV_HALO
        const = lambda b, i: (0, 0)
        in_specs = [pl.BlockSpec((rows, d), lambda b, i: (b * nt + i, 0)),
                    pl.BlockSpec((CONV_HALO, d), lambda b, i: (jnp.maximum((b * nt + i) * per - 1, 0), 0))]
        out_spec = pl.BlockSpec((rows, d), lambda b, i: (b * nt + i, 0))
        body = functools.partial(_conv_prompt_kernel, rows=rows)
        args = [x, x]
        sem = ("parallel", "arbitrary")
        name = "conv_prompt"
    else:
        rows = seq
        grid = (batch,)
        const = lambda b: (0, 0)
        in_specs = [pl.BlockSpec((rows, d), lambda b: (b, 0)),
                    pl.BlockSpec((1, CONV_WIDTH - 1, d), lambda b: (b, 0, 0))]
        out_spec = pl.BlockSpec((rows, d), lambda b: (b, 0))
        body = functools.partial(_conv_sample_kernel, rows=rows)
        args = [x, state]
        sem = ("parallel",)
        name = "conv_sample"
    in_specs += [pl.BlockSpec((CONV_WIDTH, d), const)] + [pl.BlockSpec((1, d), const)] * 3
    return pl.pallas_call(
        body,
        grid=grid,
        in_specs=in_specs,
        out_specs=out_spec,
        out_shape=jax.ShapeDtypeStruct(x.shape, _BF16),
        scratch_shapes=[pltpu.VMEM((CONV_HALO + rows, d), _F32), pltpu.VMEM((rows, d), _F32)],
        compiler_params=_params(*sem),
        name=name,
    )(*args, w_dw, vec(b_dw), vec(ln_g), vec(ln_b))


def _top_rows(s, k):
    n = s.shape[0]
    iota = lax.broadcasted_iota(jnp.int32, s.shape, 0)
    vals, idxs = [], []
    for _ in range(k):
        m = jnp.max(s, axis=0, keepdims=True)
        am = jnp.min(jnp.where(s == m, iota, n), axis=0, keepdims=True)
        vals.append(m)
        idxs.append(am)
        s = jnp.where(iota == am, -jnp.inf, s)
    return jnp.concatenate(vals, axis=0), jnp.concatenate(idxs, axis=0)


def _router_kernel(q_ref, keys_ref, a_ref, b_ref, g_ref, *, half):
    kk = PEER_TOPK
    q = q_ref[...]
    tops = []
    for c in range(2):
        s = lax.dot_general(keys_ref[0, c], q[:, c * half:(c + 1) * half], _NT, preferred_element_type=_F32)
        tops.append(_top_rows(s, kk))
    (s0, i0), (s1, i1) = tops
    t = s0.shape[1]
    cand = jnp.concatenate([s0[k1:k1 + 1] + s1 for k1 in range(kk)], axis=0)
    row_a = jnp.concatenate([jnp.broadcast_to(i0[k1:k1 + 1], (kk, t)) for k1 in range(kk)], axis=0)
    row_b = jnp.concatenate([i1] * kk, axis=0)
    iota = lax.broadcasted_iota(jnp.int32, cand.shape, 0)
    best, sel_a, sel_b = [], [], []
    for _ in range(kk):
        m = jnp.max(cand, axis=0, keepdims=True)
        pos = jnp.min(jnp.where(cand == m, iota, kk * kk), axis=0, keepdims=True)
        hit = iota == pos
        best.append(m)
        sel_a.append(jnp.max(jnp.where(hit, row_a, -1), axis=0, keepdims=True))
        sel_b.append(jnp.max(jnp.where(hit, row_b, -1), axis=0, keepdims=True))
        cand = jnp.where(hit, -jnp.inf, cand)
    best = jnp.concatenate(best, axis=0)
    e = jnp.exp(best - best[0:1])
    g_ref[...] = e / jnp.sum(e, axis=0, keepdims=True)
    a_ref[...] = jnp.concatenate(sel_a, axis=0)
    b_ref[...] = jnp.concatenate(sel_b, axis=0)


def _peer_route(q, sub_keys, *, tm=256):
    n = q.shape[0]
    heads, _, n_keys, half = sub_keys.shape
    nsel = heads * PEER_TOPK
    out_spec = pl.BlockSpec((PEER_TOPK, tm), lambda i, h: (h, i))
    return pl.pallas_call(
        functools.partial(_router_kernel, half=half),
        grid=(n // tm, heads),
        in_specs=[pl.BlockSpec((tm, 2 * half), lambda i, h: (i, h)),
                  pl.BlockSpec((1, 2, n_keys, half), lambda i, h: (h, 0, 0, 0))],
        out_specs=[out_spec, out_spec, out_spec],
        out_shape=[jax.ShapeDtypeStruct((nsel, n), jnp.int32), jax.ShapeDtypeStruct((nsel, n), jnp.int32),
                   jax.ShapeDtypeStruct((nsel, n), _F32)],
        compiler_params=_params("parallel", "parallel"),
        name="peer_route",
    )(q, sub_keys)


def _gate_matrix_kernel(a_ref, b_ref, g_ref, o_ref, at_ref, bt_ref, gt_ref, s_ref, *, n_keys, tg):
    at_ref[...] = a_ref[...].T
    bt_ref[...] = b_ref[...].T
    gt_ref[...] = g_ref[...].T
    nsel = at_ref.shape[1]
    key_id = lax.broadcasted_iota(jnp.int32, (n_keys, nsel), 0)

    def token(t, carry):
        row = pl.ds(t, 1)
        lhs = jnp.where(at_ref[row, :] == key_id, gt_ref[row, :], 0.0).astype(_BF16)
        rhs = jnp.where(bt_ref[row, :] == key_id, 1.0, 0.0).astype(_BF16)
        s_ref[pl.ds(pl.multiple_of(t * n_keys, n_keys), n_keys), :] = lax.dot_general(
            lhs, rhs, _NT, preferred_element_type=_F32)
        return carry

    lax.fori_loop(0, tg, token, 0)
    for i in range(n_keys):
        o_ref[:, i * n_keys:(i + 1) * n_keys] = s_ref[pl.ds(i, tg, stride=n_keys), :].astype(o_ref.dtype)


def _gate_matrix(sel_a, sel_b, gate, *, n_keys, tg=128):
    nsel, n = sel_a.shape
    in_spec = pl.BlockSpec((nsel, tg), lambda i: (0, i))
    return pl.pallas_call(
        functools.partial(_gate_matrix_kernel, n_keys=n_keys, tg=tg),
        grid=(n // tg,),
        in_specs=[in_spec, in_spec, in_spec],
        out_specs=pl.BlockSpec((tg, n_keys * n_keys), lambda i: (i, 0)),
        out_shape=jax.ShapeDtypeStruct((n, n_keys * n_keys), _BF16),
        scratch_shapes=[pltpu.VMEM((tg, nsel), jnp.int32), pltpu.VMEM((tg, nsel), jnp.int32),
                        pltpu.VMEM((tg, nsel), _F32), pltpu.VMEM((tg * n_keys, n_keys), _F32)],
        compiler_params=_params("parallel"),
        name="peer_gates",
    )(sel_a, sel_b, gate)


def _peer_dense_kernel(x_ref, u_ref, v_ref, gates_ref, h_ref, o_ref):
    j = pl.program_id(1)
    act = lax.dot_general(x_ref[...], u_ref[...], _NT, preferred_element_type=_F32)
    gelu = 0.5 * act * (1.0 + lax.erf(act * (2.0 ** -0.5)))
    w = (gelu * gates_ref[...].astype(_F32)).astype(_BF16)
    y = jnp.dot(w, v_ref[...], preferred_element_type=_F32)

    @pl.when(j == 0)
    def _():
        o_ref[...] = h_ref[...] + y

    @pl.when(j > 0)
    def _():
        o_ref[...] += y


def _peer_dense(x, u, v, gates, h, *, tm=512, te=256):
    n, d = x.shape
    n_exp = u.shape[0]
    tm = min(tm, n)
    once = dict(pipeline_mode=pl.Buffered(1))
    return pl.pallas_call(
        _peer_dense_kernel,
        grid=(n // tm, n_exp // te),
        in_specs=[pl.BlockSpec((tm, d), lambda i, j: (i, 0), **once),
                  pl.BlockSpec((te, d), lambda i, j: (j, 0)),
                  pl.BlockSpec((te, d), lambda i, j: (j, 0)),
                  pl.BlockSpec((tm, te), lambda i, j: (i, j)),
                  pl.BlockSpec((tm, d), lambda i, j: (i, 0), **once)],
        out_specs=pl.BlockSpec((tm, d), lambda i, j: (i, 0)),
        out_shape=jax.ShapeDtypeStruct((n, d), _F32),
        compiler_params=_params("parallel", "arbitrary"),
        name="peer_dense",
    )(x, u, v, gates, h)


def _peer(h, norm_g, w_q, sub_keys, u_tab, v_tab):
    assert h.shape[0] % LANES == 0
    n_keys = sub_keys.shape[2]
    hn = _rmsnorm(h, norm_g, _BF16)
    q = _matmul(hn, w_q, out_dtypes=(_BF16,), name="peer_query")
    sel_a, sel_b, gate = _peer_route(q, sub_keys)
    gates = _gate_matrix(sel_a, sel_b, gate, n_keys=n_keys)
    return _peer_dense(hn, u_tab, v_tab, gates, h)


def kernel(x_prompt, x_sample, cache_k, cache_v, state_conv, mixer_norm_g, ffn_norm_g, final_norm_g, w_qkv, lambda_q1, lambda_k1, lambda_q2, lambda_k2, subln_g, w_o, w_pw1, b_pw1, w_dw, b_dw, conv_ln_g, conv_ln_b, w_pw2, b_pw2, peer_wq, peer_sub_keys, peer_u, peer_v):
    batch, seq, d = x_prompt.shape
    dbatch, dseq, _ = x_sample.shape
    depth = mixer_norm_g.shape[0]
    n_heads, dh = cache_k.shape[3], cache_k.shape[5]
    past = cache_k.shape[2]
    assert dseq == CHUNK and past % CHUNK == 0 and seq % 256 == 0
    sets = [dict(h=x_prompt.reshape(batch * seq, d), batch=batch, seq=seq, prompt=True),
            dict(h=x_sample.reshape(dbatch * dseq, d), batch=dbatch, seq=dseq, prompt=False)]
    new_k, new_v, new_conv = [[], []], [[], []], [[], []]
    bf = lambda a: a.astype(_BF16)

    for i in range(depth):
        if i % 2 == 0:
            a = i // 2
            lam_init = 0.8 - 0.6 * math.exp(-0.3 * i)
            wq, wk, wv = (bf(w_qkv[a][:, s * d:(s + 1) * d]) for s in range(3))
            wo = bf(w_o[a])
            lam_params = [p[a].reshape(1, dh) for p in (lambda_q1, lambda_k1, lambda_q2, lambda_k2)]
            for si, st in enumerate(sets):
                hn = _rmsnorm(st["h"], mixer_norm_g[i], _BF16)
                q = _matmul(hn, wq, out_dtypes=(_BF16,), name="q_proj")
                k32, k16 = _matmul(hn, wk, out_dtypes=(_F32, _BF16), name="k_proj")
                v32, v16 = _matmul(hn, wv, out_dtypes=(_F32, _BF16), name="v_proj")
                kw = dict(batch=st["batch"], seq=st["seq"], n_heads=n_heads, dh=dh, lam_init=lam_init)
                if st["prompt"]:
                    o = _attn_prompt(q, k16, v16, lam_params, subln_g[a], **kw)
                else:
                    o = _attn_sample(q, k16, v16, cache_k[a].reshape(dbatch, past, d),
                                     cache_v[a].reshape(dbatch, past, d), lam_params, subln_g[a], **kw)
                st["h"] = _matmul(o, wo, resid=st["h"], name="o_proj")
                new_k[si].append(k32.reshape(st["batch"], st["seq"], n_heads, 2, dh))
                new_v[si].append(v32.reshape(st["batch"], st["seq"], n_heads, 2 * dh))
        else:
            c = i // 2
            w1, w2 = bf(w_pw1[c]), bf(w_pw2[c])
            for si, st in enumerate(sets):
                hn = _rmsnorm(st["h"], mixer_norm_g[i], _BF16)
                glu = _glu_matmul(hn, w1, b_pw1[c])
                state = None if st["prompt"] else state_conv[c]
                z = _conv_module(glu, state, w_dw[c], b_dw[c], conv_ln_g[c], conv_ln_b[c],
                                 batch=st["batch"], seq=st["seq"])
                st["h"] = _matmul(z, w2, bias=b_pw2[c], resid=st["h"], name="pw2")
                glu3 = glu.reshape(st["batch"], st["seq"], d)
                if st["prompt"]:
                    new_conv[si].append(glu3[:, seq - (CONV_WIDTH - 1):])
                else:
                    ctx = jnp.concatenate([state, glu3], axis=1)
                    new_conv[si].append(ctx[:, -(CONV_WIDTH - 1):])
        wpq, keys, u_tab, v_tab = bf(peer_wq[i]), bf(peer_sub_keys[i]), bf(peer_u[i]), bf(peer_v[i])
        for st in sets:
            st["h"] = _peer(st["h"], ffn_norm_g[i], wpq, keys, u_tab, v_tab)

    y_prompt = _rmsnorm(sets[0]["h"], final_norm_g, _F32).reshape(batch, seq, d)
    y_sample = _rmsnorm(sets[1]["h"], final_norm_g, _F32).reshape(dbatch, dseq, d)
    return (y_prompt, y_sample, jnp.stack(new_k[0]), jnp.stack(new_v[0]), jnp.stack(new_conv[0]),
            jnp.stack(new_k[1]), jnp.stack(new_v[1]), jnp.stack(new_conv[1]))
```

```python
import functools
import math

import jax
import jax.numpy as jnp
from jax import lax
from jax.experimental import pallas as pl
from jax.experimental.pallas import tpu as pltpu

EPS = 1e-6
CHUNK = 64
CONV_WIDTH = 31
CONV_HALO = 32
PEER_TOPK = 16
LANES = 128
SUBLANES = 8
VMEM_LIMIT = 48 * 1024 * 1024

_NT = (((1,), (1,)), ((), ()))
_F32 = jnp.float32
_BF16 = jnp.bfloat16


def _params(*sem):
    return pltpu.CompilerParams(dimension_semantics=sem, vmem_limit_bytes=VMEM_LIMIT)


def _cast_kernel(x_ref, o_ref):
    o_ref[...] = x_ref[0].astype(o_ref.dtype)


def _layer_bf16(w, layer, col0=0, ncols=None):
    _, rows, cols = w.shape
    ncols = cols if ncols is None else ncols
    tr, tc = min(512, rows), min(2048, ncols)
    assert col0 % tc == 0
    c0 = col0 // tc
    return pl.pallas_call(
        _cast_kernel,
        grid=(rows // tr, ncols // tc),
        in_specs=[pl.BlockSpec((1, tr, tc), lambda i, j: (layer, i, j + c0))],
        out_specs=pl.BlockSpec((tr, tc), lambda i, j: (i, j)),
        out_shape=jax.ShapeDtypeStruct((rows, ncols), _BF16),
        compiler_params=_params("parallel", "parallel"),
        name="to_bf16",
    )(w)


def _rmsnorm_kernel(x_ref, g_ref, o_ref):
    x = x_ref[...]
    ms = jnp.mean(x * x, axis=-1, keepdims=True)
    o_ref[...] = (x * lax.rsqrt(ms + EPS) * g_ref[...]).astype(o_ref.dtype)


def _rmsnorm(x, g, out_dtype, tr=256):
    m, d = x.shape
    return pl.pallas_call(
        _rmsnorm_kernel,
        grid=(m // tr,),
        in_specs=[pl.BlockSpec((tr, d), lambda i: (i, 0)), pl.BlockSpec((1, d), lambda i: (0, 0))],
        out_specs=pl.BlockSpec((tr, d), lambda i: (i, 0)),
        out_shape=jax.ShapeDtypeStruct((m, d), out_dtype),
        compiler_params=_params("parallel"),
        name="rmsnorm",
    )(x, g.reshape(1, d))


def _mm_kernel(*refs, has_bias, has_resid, n_out, multi_k):
    x_ref, w_ref = refs[0], refs[1]
    pos = 2
    b_ref = r_ref = None
    if has_bias:
        b_ref = refs[pos]
        pos += 1
    if has_resid:
        r_ref = refs[pos]
        pos += 1
    out_refs = refs[pos:pos + n_out]

    def finish(r):
        if has_bias:
            r = r + b_ref[...]
        if has_resid:
            r = r + r_ref[...]
        for o in out_refs:
            o[...] = r.astype(o.dtype)

    if not multi_k:
        finish(jnp.dot(x_ref[...], w_ref[...], preferred_element_type=_F32))
        return
    acc_ref = refs[pos + n_out]
    k = pl.program_id(2)

    @pl.when(k == 0)
    def _():
        acc_ref[...] = jnp.dot(x_ref[...], w_ref[...], preferred_element_type=_F32)

    @pl.when(k > 0)
    def _():
        acc_ref[...] += jnp.dot(x_ref[...], w_ref[...], preferred_element_type=_F32)

    @pl.when(k == pl.num_programs(2) - 1)
    def _():
        finish(acc_ref[...])


def _mm_tiles(m, n, k):
    return min(1024, m), min(512, n), min(4096, k)


def _matmul(x, w, *, bias=None, resid=None, out_dtypes=(_F32,), name="matmul"):
    m, kdim = x.shape
    n = w.shape[1]
    tm, tn, tk = _mm_tiles(m, n, kdim)
    multi_k = kdim > tk
    in_specs = [pl.BlockSpec((tm, tk), lambda i, j, k: (i, k)),
                pl.BlockSpec((tk, tn), lambda i, j, k: (k, j))]
    args = [x, w]
    if bias is not None:
        in_specs.append(pl.BlockSpec((1, tn), lambda i, j, k: (0, j)))
        args.append(bias.reshape(1, n))
    if resid is not None:
        in_specs.append(pl.BlockSpec((tm, tn), lambda i, j, k: (i, j)))
        args.append(resid)
    outs = pl.pallas_call(
        functools.partial(_mm_kernel, has_bias=bias is not None, has_resid=resid is not None,
                          n_out=len(out_dtypes), multi_k=multi_k),
        grid=(m // tm, n // tn, kdim // tk),
        in_specs=in_specs,
        out_specs=[pl.BlockSpec((tm, tn), lambda i, j, k: (i, j)) for _ in out_dtypes],
        out_shape=[jax.ShapeDtypeStruct((m, n), dt) for dt in out_dtypes],
        scratch_shapes=[pltpu.VMEM((tm, tn), _F32)] if multi_k else [],
        compiler_params=_params("parallel", "parallel", "arbitrary"),
        name=name,
    )(*args)
    return outs if len(out_dtypes) > 1 else outs[0]


def _glu_kernel(x_ref, wa_ref, wg_ref, ba_ref, bg_ref, o_ref):
    x = x_ref[...]
    a = jnp.dot(x, wa_ref[...], preferred_element_type=_F32) + ba_ref[...]
    gate = jnp.dot(x, wg_ref[...], preferred_element_type=_F32) + bg_ref[...]
    o_ref[...] = a * jax.nn.sigmoid(gate)


def _glu_matmul(x, w, b):
    m, kdim = x.shape
    n = w.shape[1] // 2
    tm, tn, tk = _mm_tiles(m, n, kdim)
    assert tk == kdim
    nj = n // tn
    b2 = b.reshape(1, 2 * n)
    return pl.pallas_call(
        _glu_kernel,
        grid=(m // tm, nj),
        in_specs=[pl.BlockSpec((tm, kdim), lambda i, j: (i, 0)),
                  pl.BlockSpec((kdim, tn), lambda i, j: (0, j)),
                  pl.BlockSpec((kdim, tn), lambda i, j: (0, j + nj)),
                  pl.BlockSpec((1, tn), lambda i, j: (0, j)),
                  pl.BlockSpec((1, tn), lambda i, j: (0, j + nj))],
        out_specs=pl.BlockSpec((tm, tn), lambda i, j: (i, j)),
        out_shape=jax.ShapeDtypeStruct((m, n), _F32),
        compiler_params=_params("parallel", "parallel"),
        name="glu_matmul",
    )(x, w, w, b2, b2)


def _lambda(lq1_ref, lk1_ref, lq2_ref, lk2_ref, lam_init):
    s1 = jnp.sum(lq1_ref[...] * lk1_ref[...], axis=-1, keepdims=True)
    s2 = jnp.sum(lq2_ref[...] * lk2_ref[...], axis=-1, keepdims=True)
    return jnp.exp(s1) - jnp.exp(s2) + lam_init


def _sub_layernorm(o, g_ref, lam_init):
    ms = jnp.mean(o * o, axis=-1, keepdims=True)
    return (o * lax.rsqrt(ms + EPS)) * g_ref[...] * (1.0 - lam_init)


def _attn_prompt_kernel(lq1_ref, lk1_ref, lq2_ref, lk2_ref, g_ref, q_ref, k_ref, v_ref, o_ref,
                        s_ref, mx_ref, l_ref, acc_ref, *, tq, dh, lam_init):
    qi = pl.program_id(2)
    scale = dh ** -0.5
    q = q_ref[...]
    mx_ref[...] = jnp.full_like(mx_ref, -jnp.inf)
    l_ref[...] = jnp.zeros_like(l_ref)
    acc_ref[...] = jnp.zeros_like(acc_ref)

    def scores(kj, mask):
        off = pl.multiple_of(kj * tq, tq)
        k = k_ref[pl.ds(off, tq), :]
        for c in range(2):
            s = lax.dot_general(q[:, c * dh:(c + 1) * dh], k[:, c * dh:(c + 1) * dh], _NT,
                                preferred_element_type=_F32) * scale
            if mask is not None:
                s = jnp.where(mask, s, -jnp.inf)
            rows = pl.ds(c * tq, tq)
            s_ref[rows, pl.ds(off, tq)] = s
            part = s[:, :LANES]
            for j in range(1, tq // LANES):
                part = jnp.maximum(part, s[:, j * LANES:(j + 1) * LANES])
            mx_ref[rows, :] = jnp.maximum(mx_ref[rows, :], part)

    def scores_body(kj, carry):
        scores(kj, None)
        return carry

    lax.fori_loop(0, qi, scores_body, 0)
    row_chunk = lax.broadcasted_iota(jnp.int32, (tq, tq), 0) // CHUNK
    col_chunk = lax.broadcasted_iota(jnp.int32, (tq, tq), 1) // CHUNK
    scores(qi, col_chunk <= row_chunk)

    m = jnp.broadcast_to(jnp.max(mx_ref[...], axis=-1, keepdims=True), mx_ref.shape)
    mx_ref[...] = m

    def weigh(kj, carry):
        off = pl.multiple_of(kj * tq, tq)
        row_max = mx_ref[...]
        ps = []
        part = None
        for j in range(tq // LANES):
            p = jnp.exp(s_ref[:, pl.ds(pl.multiple_of(off + j * LANES, LANES), LANES)] - row_max)
            part = p if part is None else part + p
            ps.append(p.astype(_BF16))
        l_ref[...] += part
        acc_ref[...] += jnp.dot(jnp.concatenate(ps, axis=1), v_ref[pl.ds(off, tq), :],
                                preferred_element_type=_F32)
        return carry

    lax.fori_loop(0, qi + 1, weigh, 0)
    lam = _lambda(lq1_ref, lk1_ref, lq2_ref, lk2_ref, lam_init)
    o = acc_ref[...] / jnp.sum(l_ref[...], axis=-1, keepdims=True)
    o = o[:tq] - lam * o[tq:]
    o_ref[...] = _sub_layernorm(o, g_ref, lam_init).astype(o_ref.dtype)


def _attn_prompt(q, k, v, lam_params, subln_g, *, batch, seq, n_heads, dh, lam_init, tq=512):
    hw = 2 * dh
    assert seq % tq == 0 and tq % CHUNK == 0
    nq = seq // tq
    small = pl.BlockSpec((1, dh), lambda b, h, i: (0, 0))
    return pl.pallas_call(
        functools.partial(_attn_prompt_kernel, tq=tq, dh=dh, lam_init=lam_init),
        grid=(batch, n_heads, nq),
        in_specs=[small, small, small, small,
                  pl.BlockSpec((1, hw), lambda b, h, i: (0, 0)),
                  pl.BlockSpec((tq, hw), lambda b, h, i: (b * nq + i, h)),
                  pl.BlockSpec((seq, hw), lambda b, h, i: (b, h)),
                  pl.BlockSpec((seq, hw), lambda b, h, i: (b, h))],
        out_specs=pl.BlockSpec((tq, hw), lambda b, h, i: (b * nq + i, h)),
        out_shape=jax.ShapeDtypeStruct(q.shape, _BF16),
        scratch_shapes=[pltpu.VMEM((2 * tq, seq), _F32), pltpu.VMEM((2 * tq, LANES), _F32),
                        pltpu.VMEM((2 * tq, LANES), _F32), pltpu.VMEM((2 * tq, hw), _F32)],
        compiler_params=_params("parallel", "parallel", "arbitrary"),
        name="attn_prompt",
    )(*lam_params, subln_g.reshape(1, hw), q, k, v)


def _attn_sample_kernel(lq1_ref, lk1_ref, lq2_ref, lk2_ref, g_ref, q_ref, kn_ref, vn_ref, ck_ref, cv_ref,
                        o_ref, *, dh, lam_init):
    scale = dh ** -0.5
    q = q_ref[...]
    kn = kn_ref[...]
    kp = ck_ref[0].astype(_BF16)
    lam = _lambda(lq1_ref, lk1_ref, lq2_ref, lk2_ref, lam_init)
    a_p = a_n = None
    for c in range(2):
        qc = q[:, c * dh:(c + 1) * dh]
        s_p = lax.dot_general(qc, kp[:, c * dh:(c + 1) * dh], _NT, preferred_element_type=_F32) * scale
        s_n = lax.dot_general(qc, kn[:, c * dh:(c + 1) * dh], _NT, preferred_element_type=_F32) * scale
        m = jnp.maximum(jnp.max(s_p, axis=-1, keepdims=True), jnp.max(s_n, axis=-1, keepdims=True))
        e_p = jnp.exp(s_p - m)
        e_n = jnp.exp(s_n - m)
        denom = jnp.sum(e_p, axis=-1, keepdims=True) + jnp.sum(e_n, axis=-1, keepdims=True)
        p_p = e_p / denom
        p_n = e_n / denom
        if c == 0:
            a_p, a_n = p_p, p_n
        else:
            a_p, a_n = a_p - lam * p_p, a_n - lam * p_n
    o = (jnp.dot(a_p.astype(_BF16), cv_ref[0].astype(_BF16), preferred_element_type=_F32)
         + jnp.dot(a_n.astype(_BF16), vn_ref[...], preferred_element_type=_F32))
    o_ref[...] = _sub_layernorm(o, g_ref, lam_init).astype(o_ref.dtype)


def _attn_sample(q, k, v, cache_k, cache_v, lam_params, subln_g, *, batch, seq, n_heads, dh, lam_init):
    hw = 2 * dh
    past = cache_k.shape[1]
    small = pl.BlockSpec((1, dh), lambda b, h: (0, 0))
    new = pl.BlockSpec((seq, hw), lambda b, h: (b, h))
    old = pl.BlockSpec((1, past, hw), lambda b, h: (b, 0, h))
    return pl.pallas_call(
        functools.partial(_attn_sample_kernel, dh=dh, lam_init=lam_init),
        grid=(batch, n_heads),
        in_specs=[small, small, small, small, pl.BlockSpec((1, hw), lambda b, h: (0, 0)),
                  new, new, new, old, old],
        out_specs=new,
        out_shape=jax.ShapeDtypeStruct(q.shape, _BF16),
        compiler_params=_params("parallel", "parallel"),
        name="attn_sample",
    )(*lam_params, subln_g.reshape(1, hw), q, k, v, cache_k, cache_v)


def _conv_ln_silu(ext_ref, c_ref, w_ref, bdw_ref, g_ref, b_ref, o_ref, *, rows):
    nslab = c_ref.shape[0]
    d = nslab * LANES
    first = CONV_HALO - (CONV_WIDTH - 1)

    def slab(si, carry):
        lanes = pl.ds(pl.multiple_of(si * LANES, LANES), LANES)
        taps = [jnp.broadcast_to(w_ref[pl.ds(j, 1), lanes], (SUBLANES, LANES)) for j in range(CONV_WIDTH)]
        bias = jnp.broadcast_to(bdw_ref[:, lanes], (SUBLANES, LANES))
        for r0 in range(0, rows, 2 * SUBLANES):
            for parity in range(2):
                acc = bias
                for j in range(CONV_WIDTH):
                    acc = acc + taps[j] * ext_ref[si, pl.ds(first + r0 + parity + j, SUBLANES, stride=2), :]
                c_ref[si, pl.ds(r0 + parity, SUBLANES, stride=2), :] = acc
        return carry

    lax.fori_loop(0, nslab, slab, 0)

    total = c_ref[0]
    for si in range(1, nslab):
        total = total + c_ref[si]
    mu = jnp.broadcast_to(jnp.sum(total, axis=-1, keepdims=True) * (1.0 / d), (rows, LANES))
    total = None
    for si in range(nslab):
        cc = c_ref[si] - mu
        total = cc * cc if total is None else total + cc * cc
    var = jnp.sum(total, axis=-1, keepdims=True) * (1.0 / d)
    inv = jnp.broadcast_to(lax.rsqrt(var + EPS), (rows, LANES))
    for si in range(nslab):
        lanes = slice(si * LANES, (si + 1) * LANES)
        y = (c_ref[si] - mu) * inv * g_ref[:, lanes] + b_ref[:, lanes]
        o_ref[:, lanes] = (y * jax.nn.sigmoid(y)).astype(o_ref.dtype)


def _conv_prompt_kernel(x_ref, halo_ref, w_ref, bdw_ref, g_ref, b_ref, o_ref, ext_ref, c_ref, *, rows):
    i = pl.program_id(1)
    for si in range(ext_ref.shape[0]):
        lanes = slice(si * LANES, (si + 1) * LANES)
        halo = halo_ref[:, lanes]
        ext_ref[si, pl.ds(0, CONV_HALO), :] = jnp.where(i == 0, jnp.zeros_like(halo), halo)
        ext_ref[si, pl.ds(CONV_HALO, rows), :] = x_ref[:, lanes]
    _conv_ln_silu(ext_ref, c_ref, w_ref, bdw_ref, g_ref, b_ref, o_ref, rows=rows)


def _conv_sample_kernel(x_ref, st_ref, w_ref, bdw_ref, g_ref, b_ref, o_ref, ext_ref, c_ref, *, rows):
    ctx = CONV_WIDTH - 1
    for si in range(ext_ref.shape[0]):
        lanes = slice(si * LANES, (si + 1) * LANES)
        ext_ref[si, pl.ds(CONV_HALO - ctx, ctx), :] = st_ref[0, :, lanes]
        ext_ref[si, pl.ds(CONV_HALO, rows), :] = x_ref[:, lanes]
    _conv_ln_silu(ext_ref, c_ref, w_ref, bdw_ref, g_ref, b_ref, o_ref, rows=rows)


def _conv_module(x, state, w_dw, b_dw, ln_g, ln_b, *, batch, seq):
    d = x.shape[1]
    vec = lambda a: a.reshape(1, d)
    if state is None:
        rows = 128
        nt = seq // rows
        grid = (batch, nt)
        per = rows // CONV_HALO
        const = lambda b, i: (0, 0)
        in_specs = [pl.BlockSpec((rows, d), lambda b, i: (b * nt + i, 0)),
                    pl.BlockSpec((CONV_HALO, d), lambda b, i: (jnp.maximum((b * nt + i) * per - 1, 0), 0))]
        out_spec = pl.BlockSpec((rows, d), lambda b, i: (b * nt + i, 0))
        body = functools.partial(_conv_prompt_kernel, rows=rows)
        args = [x, x]
        sem = ("parallel", "arbitrary")
        name = "conv_prompt"
    else:
        rows = seq
        grid = (batch,)
        const = lambda b: (0, 0)
        in_specs = [pl.BlockSpec((rows, d), lambda b: (b, 0)),
                    pl.BlockSpec((1, CONV_WIDTH - 1, d), lambda b: (b, 0, 0))]
        out_spec = pl.BlockSpec((rows, d), lambda b: (b, 0))
        body = functools.partial(_conv_sample_kernel, rows=rows)
        args = [x, state]
        sem = ("parallel",)
        name = "conv_sample"
    in_specs += [pl.BlockSpec((CONV_WIDTH, d), const)] + [pl.BlockSpec((1, d), const)] * 3
    return pl.pallas_call(
        body,
        grid=grid,
        in_specs=in_specs,
        out_specs=out_spec,
        out_shape=jax.ShapeDtypeStruct(x.shape, _BF16),
        scratch_shapes=[pltpu.VMEM((d // LANES, CONV_HALO + rows, LANES), _F32),
                        pltpu.VMEM((d // LANES, rows, LANES), _F32)],
        compiler_params=_params(*sem),
        name=name,
    )(*args, w_dw, vec(b_dw), vec(ln_g), vec(ln_b))


def _top_rows(s, k):
    n = s.shape[0]
    iota = lax.broadcasted_iota(jnp.int32, s.shape, 0).astype(_F32)
    vals, idxs = [], []
    for _ in range(k):
        m = jnp.max(s, axis=0, keepdims=True)
        am = jnp.min(jnp.where(s == m, iota, float(n)), axis=0, keepdims=True)
        vals.append(m)
        idxs.append(am)
        s = jnp.where(iota == am, -jnp.inf, s)
    return jnp.concatenate(vals, axis=0), jnp.concatenate(idxs, axis=0)


def _pick_row(rank, table):
    out = jnp.zeros_like(rank)
    for k in range(table.shape[0]):
        out = jnp.where(rank == float(k), table[k:k + 1], out)
    return out


def _router_kernel(q_ref, keys_ref, a_ref, b_ref, g_ref, *, half):
    kk = PEER_TOPK
    lim = 4
    assert lim * lim >= kk
    q = q_ref[...]
    tops = []
    for c in range(2):
        s = lax.dot_general(keys_ref[0, c], q[:, c * half:(c + 1) * half], _NT, preferred_element_type=_F32)
        tops.append(_top_rows(s, kk))
    (s0, i0), (s1, i1) = tops
    rank = lax.broadcasted_iota(jnp.int32, s0.shape, 0)
    rank_f = rank.astype(_F32)
    never = float(kk * kk)
    cands, flats = [], []
    for k1 in range(lim):
        ok = rank < kk // (k1 + 1)
        cands.append(jnp.where(ok, s0[k1:k1 + 1] + s1, -jnp.inf))
        flats.append(jnp.where(ok, k1 * kk + rank_f, never))
    for k2 in range(lim):
        if kk // (k2 + 1) <= lim:
            continue
        ok = (rank >= lim) & (rank < kk // (k2 + 1))
        cands.append(jnp.where(ok, s0 + s1[k2:k2 + 1], -jnp.inf))
        flats.append(jnp.where(ok, rank_f * kk + k2, never))
    cand = jnp.concatenate(cands, axis=0)
    flat = jnp.concatenate(flats, axis=0)
    best, pos = [], []
    for _ in range(kk):
        m = jnp.max(cand, axis=0, keepdims=True)
        p = jnp.min(jnp.where(cand == m, flat, never), axis=0, keepdims=True)
        best.append(m)
        pos.append(p)
        cand = jnp.where(flat == p, -jnp.inf, cand)
    best = jnp.concatenate(best, axis=0)
    pos = jnp.concatenate(pos, axis=0)
    k1 = jnp.floor(pos * (1.0 / kk))
    k2 = pos - k1 * kk
    e = jnp.exp(best - best[0:1])
    g_ref[...] = e / jnp.sum(e, axis=0, keepdims=True)
    a_ref[...] = _pick_row(k1, i0).astype(jnp.int32)
    b_ref[...] = _pick_row(k2, i1).astype(jnp.int32)


def _peer_route(q, sub_keys, *, tm=256):
    n = q.shape[0]
    heads, _, n_keys, half = sub_keys.shape
    nsel = heads * PEER_TOPK
    out_spec = pl.BlockSpec((PEER_TOPK, tm), lambda i, h: (h, i))
    return pl.pallas_call(
        functools.partial(_router_kernel, half=half),
        grid=(n // tm, heads),
        in_specs=[pl.BlockSpec((tm, 2 * half), lambda i, h: (i, h)),
                  pl.BlockSpec((1, 2, n_keys, half), lambda i, h: (h, 0, 0, 0))],
        out_specs=[out_spec, out_spec, out_spec],
        out_shape=[jax.ShapeDtypeStruct((nsel, n), jnp.int32), jax.ShapeDtypeStruct((nsel, n), jnp.int32),
                   jax.ShapeDtypeStruct((nsel, n), _F32)],
        compiler_params=_params("parallel", "parallel"),
        name="peer_route",
    )(q, sub_keys)


def _gate_matrix_kernel(a_ref, b_ref, g_ref, o_ref, at_ref, bt_ref, gt_ref, s_ref, *, n_keys, tg):
    at_ref[...] = a_ref[...].T
    bt_ref[...] = b_ref[...].T
    gt_ref[...] = g_ref[...].T
    nsel = at_ref.shape[1]
    pitch = _gate_pitch(n_keys)
    key_id = lax.broadcasted_iota(jnp.int32, (n_keys, nsel), 0)

    def token(t, carry):
        row = pl.ds(t, 1)
        lhs = jnp.where(at_ref[row, :] == key_id, gt_ref[row, :], 0.0).astype(_BF16)
        rhs = jnp.where(bt_ref[row, :] == key_id, 1.0, 0.0).astype(_BF16)
        s_ref[pl.ds(pl.multiple_of(t * pitch, SUBLANES), n_keys), :] = lax.dot_general(
            lhs, rhs, _NT, preferred_element_type=_F32)
        return carry

    lax.fori_loop(0, tg, token, 0, unroll=8)
    for i in range(n_keys):
        o_ref[:, i * n_keys:(i + 1) * n_keys] = s_ref[pl.ds(i, tg, stride=pitch), :].astype(o_ref.dtype)


def _gate_pitch(n_keys):
    return n_keys + SUBLANES if (n_keys // SUBLANES) % 2 == 0 else n_keys


def _gate_matrix(sel_a, sel_b, gate, *, n_keys, tg=128):
    nsel, n = sel_a.shape
    in_spec = pl.BlockSpec((nsel, tg), lambda i: (0, i))
    return pl.pallas_call(
        functools.partial(_gate_matrix_kernel, n_keys=n_keys, tg=tg),
        grid=(n // tg,),
        in_specs=[in_spec, in_spec, in_spec],
        out_specs=pl.BlockSpec((tg, n_keys * n_keys), lambda i: (i, 0)),
        out_shape=jax.ShapeDtypeStruct((n, n_keys * n_keys), _BF16),
        scratch_shapes=[pltpu.VMEM((tg, nsel), jnp.int32), pltpu.VMEM((tg, nsel), jnp.int32),
                        pltpu.VMEM((tg, nsel), _F32), pltpu.VMEM((tg * _gate_pitch(n_keys), n_keys), _F32)],
        compiler_params=_params("parallel"),
        name="peer_gates",
    )(sel_a, sel_b, gate)


def _peer_weights_kernel(x_ref, u_ref, gates_ref, o_ref):
    act = lax.dot_general(x_ref[...], u_ref[...], _NT, preferred_element_type=_F32)
    gelu = 0.5 * act * (1.0 + lax.erf(act * (2.0 ** -0.5)))
    o_ref[...] = (gelu * gates_ref[...].astype(_F32)).astype(o_ref.dtype)


def _peer_weights(x, u, gates, *, tm=1024, te=512):
    n, d = x.shape
    n_exp = u.shape[0]
    tm = min(tm, n)
    return pl.pallas_call(
        _peer_weights_kernel,
        grid=(n // tm, n_exp // te),
        in_specs=[pl.BlockSpec((tm, d), lambda i, j: (i, 0)),
                  pl.BlockSpec((te, d), lambda i, j: (j, 0)),
                  pl.BlockSpec((tm, te), lambda i, j: (i, j))],
        out_specs=pl.BlockSpec((tm, te), lambda i, j: (i, j)),
        out_shape=jax.ShapeDtypeStruct((n, n_exp), _BF16),
        compiler_params=_params("parallel", "parallel"),
        name="peer_weights",
    )(x, u, gates)


def _peer(h, norm_g, w_q, sub_keys, u_tab, v_tab):
    assert h.shape[0] % LANES == 0
    n_keys = sub_keys.shape[2]
    hn = _rmsnorm(h, norm_g, _BF16)
    q = _matmul(hn, w_q, out_dtypes=(_BF16,), name="peer_query")
    sel_a, sel_b, gate = _peer_route(q, sub_keys)
    gates = _gate_matrix(sel_a, sel_b, gate, n_keys=n_keys)
    weights = _peer_weights(hn, u_tab, gates)
    return _matmul(weights, v_tab, resid=h, name="peer_mix")


def kernel(x_prompt, x_sample, cache_k, cache_v, state_conv, mixer_norm_g, ffn_norm_g, final_norm_g, w_qkv, lambda_q1, lambda_k1, lambda_q2, lambda_k2, subln_g, w_o, w_pw1, b_pw1, w_dw, b_dw, conv_ln_g, conv_ln_b, w_pw2, b_pw2, peer_wq, peer_sub_keys, peer_u, peer_v):
    batch, seq, d = x_prompt.shape
    dbatch, dseq, _ = x_sample.shape
    depth = mixer_norm_g.shape[0]
    n_heads, dh = cache_k.shape[3], cache_k.shape[5]
    past = cache_k.shape[2]
    assert dseq == CHUNK and past % CHUNK == 0
    sets = [dict(h=x_prompt.reshape(batch * seq, d), batch=batch, seq=seq, prompt=True),
            dict(h=x_sample.reshape(dbatch * dseq, d), batch=dbatch, seq=dseq, prompt=False)]
    new_k, new_v, new_conv = [[], []], [[], []], [[], []]

    for i in range(depth):
        if i % 2 == 0:
            a = i // 2
            lam_init = 0.8 - 0.6 * math.exp(-0.3 * i)
            wq, wk, wv = (_layer_bf16(w_qkv, a, s * d, d) for s in range(3))
            wo = _layer_bf16(w_o, a)
            lam_params = [p[a].reshape(1, dh) for p in (lambda_q1, lambda_k1, lambda_q2, lambda_k2)]
            for si, st in enumerate(sets):
                hn = _rmsnorm(st["h"], mixer_norm_g[i], _BF16)
                q = _matmul(hn, wq, out_dtypes=(_BF16,), name="q_proj")
                k32, k16 = _matmul(hn, wk, out_dtypes=(_F32, _BF16), name="k_proj")
                v32, v16 = _matmul(hn, wv, out_dtypes=(_F32, _BF16), name="v_proj")
                kw = dict(batch=st["batch"], seq=st["seq"], n_heads=n_heads, dh=dh, lam_init=lam_init)
                if st["prompt"]:
                    o = _attn_prompt(q, k16, v16, lam_params, subln_g[a], **kw)
                else:
                    o = _attn_sample(q, k16, v16, cache_k[a].reshape(dbatch, past, d),
                                     cache_v[a].reshape(dbatch, past, d), lam_params, subln_g[a], **kw)
                st["h"] = _matmul(o, wo, resid=st["h"], name="o_proj")
                new_k[si].append(k32.reshape(st["batch"], st["seq"], n_heads, 2, dh))
                new_v[si].append(v32.reshape(st["batch"], st["seq"], n_heads, 2 * dh))
        else:
            c = i // 2
            w1, w2 = _layer_bf16(w_pw1, c), _layer_bf16(w_pw2, c)
            for si, st in enumerate(sets):
                hn = _rmsnorm(st["h"], mixer_norm_g[i], _BF16)
                glu = _glu_matmul(hn, w1, b_pw1[c])
                state = None if st["prompt"] else state_conv[c]
                z = _conv_module(glu, state, w_dw[c], b_dw[c], conv_ln_g[c], conv_ln_b[c],
                                 batch=st["batch"], seq=st["seq"])
                st["h"] = _matmul(z, w2, bias=b_pw2[c], resid=st["h"], name="pw2")
                glu3 = glu.reshape(st["batch"], st["seq"], d)
                if st["prompt"]:
                    new_conv[si].append(glu3[:, seq - (CONV_WIDTH - 1):])
                else:
                    ctx = jnp.concatenate([state, glu3], axis=1)
                    new_conv[si].append(ctx[:, -(CONV_WIDTH - 1):])
        wpq, u_tab, v_tab = _layer_bf16(peer_wq, i), _layer_bf16(peer_u, i), _layer_bf16(peer_v, i)
        keys = peer_sub_keys[i].astype(_BF16)
        for st in sets:
            st["h"] = _peer(st["h"], ffn_norm_g[i], wpq, keys, u_tab, v_tab)

    y_prompt = _rmsnorm(sets[0]["h"], final_norm_g, _F32).reshape(batch, seq, d)
    y_sample = _rmsnorm(sets[1]["h"], final_norm_g, _F32).reshape(dbatch, dseq, d)
    return (y_prompt, y_sample, jnp.stack(new_k[0]), jnp.stack(new_v[0]), jnp.stack(new_conv[0]),
            jnp.stack(new_k[1]), jnp.stack(new_v[1]), jnp.stack(new_conv[1]))
```

```python
import functools
import math

import jax
import jax.numpy as jnp
from jax import lax
from jax.experimental import pallas as pl
from jax.experimental.pallas import tpu as pltpu

EPS = 1e-6
CHUNK = 64
CONV_WIDTH = 31
CONV_HALO = 32
PEER_TOPK = 16
LANES = 128
SUBLANES = 8
VMEM_LIMIT = 56 * 1024 * 1024

_NT = (((1,), (1,)), ((), ()))
_F32 = jnp.float32
_BF16 = jnp.bfloat16


def _params(*sem):
    return pltpu.CompilerParams(dimension_semantics=sem, vmem_limit_bytes=VMEM_LIMIT)


def _cast_kernel(x_ref, o_ref):
    o_ref[...] = x_ref[0].astype(o_ref.dtype)


def _layer_bf16(w, layer, col0=0, ncols=None):
    _, rows, cols = w.shape
    ncols = cols if ncols is None else ncols
    tr, tc = min(512, rows), min(2048, ncols)
    assert col0 % tc == 0
    c0 = col0 // tc
    return pl.pallas_call(
        _cast_kernel,
        grid=(rows // tr, ncols // tc),
        in_specs=[pl.BlockSpec((1, tr, tc), lambda i, j: (layer, i, j + c0))],
        out_specs=pl.BlockSpec((tr, tc), lambda i, j: (i, j)),
        out_shape=jax.ShapeDtypeStruct((rows, ncols), _BF16),
        compiler_params=_params("parallel", "parallel"),
        name="to_bf16",
    )(w)


def _rmsnorm_kernel(x_ref, g_ref, o_ref):
    x = x_ref[...]
    ms = jnp.mean(x * x, axis=-1, keepdims=True)
    o_ref[...] = (x * lax.rsqrt(ms + EPS) * g_ref[...]).astype(o_ref.dtype)


def _rmsnorm(x, g, out_dtype, tr=256):
    m, d = x.shape
    return pl.pallas_call(
        _rmsnorm_kernel,
        grid=(m // tr,),
        in_specs=[pl.BlockSpec((tr, d), lambda i: (i, 0)), pl.BlockSpec((1, d), lambda i: (0, 0))],
        out_specs=pl.BlockSpec((tr, d), lambda i: (i, 0)),
        out_shape=jax.ShapeDtypeStruct((m, d), out_dtype),
        compiler_params=_params("parallel"),
        name="rmsnorm",
    )(x, g.reshape(1, d))


def _mm_kernel(*refs, has_bias, has_resid, n_out, multi_k):
    x_ref, w_ref = refs[0], refs[1]
    pos = 2
    b_ref = r_ref = None
    if has_bias:
        b_ref = refs[pos]
        pos += 1
    if has_resid:
        r_ref = refs[pos]
        pos += 1
    out_refs = refs[pos:pos + n_out]

    def finish(r):
        if has_bias:
            r = r + b_ref[...]
        if has_resid:
            r = r + r_ref[...]
        for o in out_refs:
            if len(o.shape) == 2:
                o[...] = r.astype(o.dtype)
            else:
                rows, groups, _ = o.shape
                flat = o.reshape(rows * groups, LANES)
                for c in range(groups):
                    flat[pl.ds(c, rows, stride=groups), :] = r[:, c * LANES:(c + 1) * LANES].astype(o.dtype)

    def product():
        return jnp.dot(x_ref[...], w_ref[...].astype(x_ref.dtype), preferred_element_type=_F32)

    if not multi_k:
        finish(product())
        return
    acc_ref = refs[pos + n_out]
    k = pl.program_id(2)

    @pl.when(k == 0)
    def _():
        acc_ref[...] = product()

    @pl.when(k > 0)
    def _():
        acc_ref[...] += product()

    @pl.when(k == pl.num_programs(2) - 1)
    def _():
        finish(acc_ref[...])


def _mm_tiles(m, n, k):
    return min(1024, m), min(512, n), min(4096, k)


def _matmul(x, w, *, w_layer=None, bias=None, resid=None, out_dtypes=(_F32,), lane_rows_first=False, tiles=None,
            name="matmul"):
    m, kdim = x.shape
    n = w.shape[-1]
    tm, tn, tk = tiles or _mm_tiles(m, n, kdim)
    tm, tn, tk = min(tm, m), min(tn, n), min(tk, kdim)
    multi_k = kdim > tk
    out_specs = [pl.BlockSpec((tm, tn), lambda i, j, k: (i, j)) for _ in out_dtypes]
    out_shape = [jax.ShapeDtypeStruct((m, n), dt) for dt in out_dtypes]
    if lane_rows_first:
        assert (tn // LANES) % SUBLANES == 0 or tn == n
        out_specs[0] = pl.BlockSpec((tm, tn // LANES, LANES), lambda i, j, k: (i, j, 0))
        out_shape[0] = jax.ShapeDtypeStruct((m, n // LANES, LANES), out_dtypes[0])
    if w_layer is None:
        w_spec = pl.BlockSpec((tk, tn), lambda i, j, k: (k, j))
    else:
        w_spec = pl.BlockSpec((None, tk, tn), lambda i, j, k: (w_layer, k, j))
    in_specs = [pl.BlockSpec((tm, tk), lambda i, j, k: (i, k)), w_spec]
    args = [x, w]
    if bias is not None:
        in_specs.append(pl.BlockSpec((1, tn), lambda i, j, k: (0, j)))
        args.append(bias.reshape(1, n))
    if resid is not None:
        in_specs.append(pl.BlockSpec((tm, tn), lambda i, j, k: (i, j)))
        args.append(resid)
    outs = pl.pallas_call(
        functools.partial(_mm_kernel, has_bias=bias is not None, has_resid=resid is not None,
                          n_out=len(out_dtypes), multi_k=multi_k),
        grid=(m // tm, n // tn, kdim // tk),
        in_specs=in_specs,
        out_specs=out_specs,
        out_shape=out_shape,
        scratch_shapes=[pltpu.VMEM((tm, tn), _F32)] if multi_k else [],
        compiler_params=_params("parallel", "parallel", "arbitrary"),
        name=name,
    )(*args)
    return outs if len(out_dtypes) > 1 else outs[0]


def _glu_kernel(x_ref, wa_ref, wg_ref, ba_ref, bg_ref, o_ref):
    x = x_ref[...]
    a = jnp.dot(x, wa_ref[...], preferred_element_type=_F32) + ba_ref[...]
    gate = jnp.dot(x, wg_ref[...], preferred_element_type=_F32) + bg_ref[...]
    o_ref[...] = a * jax.nn.sigmoid(gate)


def _glu_matmul(x, w, b):
    m, kdim = x.shape
    n = w.shape[1] // 2
    tm, tn, tk = _mm_tiles(m, n, kdim)
    assert tk == kdim
    nj = n // tn
    b2 = b.reshape(1, 2 * n)
    return pl.pallas_call(
        _glu_kernel,
        grid=(m // tm, nj),
        in_specs=[pl.BlockSpec((tm, kdim), lambda i, j: (i, 0)),
                  pl.BlockSpec((kdim, tn), lambda i, j: (0, j)),
                  pl.BlockSpec((kdim, tn), lambda i, j: (0, j + nj)),
                  pl.BlockSpec((1, tn), lambda i, j: (0, j)),
                  pl.BlockSpec((1, tn), lambda i, j: (0, j + nj))],
        out_specs=pl.BlockSpec((tm, tn), lambda i, j: (i, j)),
        out_shape=jax.ShapeDtypeStruct((m, n), _F32),
        compiler_params=_params("parallel", "parallel"),
        name="glu_matmul",
    )(x, w, w, b2, b2)


def _lambda(lq1_ref, lk1_ref, lq2_ref, lk2_ref, lam_init):
    s1 = jnp.sum(lq1_ref[...] * lk1_ref[...], axis=-1, keepdims=True)
    s2 = jnp.sum(lq2_ref[...] * lk2_ref[...], axis=-1, keepdims=True)
    return jnp.exp(s1) - jnp.exp(s2) + lam_init


def _sub_layernorm(o, g_ref, lam_init):
    ms = jnp.mean(o * o, axis=-1, keepdims=True)
    return (o * lax.rsqrt(ms + EPS)) * g_ref[...] * (1.0 - lam_init)


def _attn_prompt_kernel(lq1_ref, lk1_ref, lq2_ref, lk2_ref, g_ref, q_ref, k_ref, v_ref, o_ref,
                        s_ref, mx_ref, l_ref, acc_ref, *, tq, dh, lam_init):
    qi = pl.program_id(2)
    scale = dh ** -0.5
    q = q_ref[...]
    mx_ref[...] = jnp.full_like(mx_ref, -jnp.inf)
    l_ref[...] = jnp.zeros_like(l_ref)
    acc_ref[...] = jnp.zeros_like(acc_ref)

    def scores(kj, mask):
        off = pl.multiple_of(kj * tq, tq)
        k = k_ref[pl.ds(off, tq), :]
        for c in range(2):
            s = lax.dot_general(q[:, c * dh:(c + 1) * dh], k[:, c * dh:(c + 1) * dh], _NT,
                                preferred_element_type=_F32) * scale
            if mask is not None:
                s = jnp.where(mask, s, -jnp.inf)
            rows = pl.ds(c * tq, tq)
            s_ref[rows, pl.ds(off, tq)] = s
            part = s[:, :LANES]
            for j in range(1, tq // LANES):
                part = jnp.maximum(part, s[:, j * LANES:(j + 1) * LANES])
            mx_ref[rows, :] = jnp.maximum(mx_ref[rows, :], part)

    def scores_body(kj, carry):
        scores(kj, None)
        return carry

    lax.fori_loop(0, qi, scores_body, 0)
    row_chunk = lax.broadcasted_iota(jnp.int32, (tq, tq), 0) // CHUNK
    col_chunk = lax.broadcasted_iota(jnp.int32, (tq, tq), 1) // CHUNK
    scores(qi, col_chunk <= row_chunk)

    m = jnp.broadcast_to(jnp.max(mx_ref[...], axis=-1, keepdims=True), mx_ref.shape)
    mx_ref[...] = m

    def weigh(kj, carry):
        off = pl.multiple_of(kj * tq, tq)
        row_max = mx_ref[...]
        ps = []
        part = None
        for j in range(tq // LANES):
            p = jnp.exp(s_ref[:, pl.ds(pl.multiple_of(off + j * LANES, LANES), LANES)] - row_max)
            part = p if part is None else part + p
            ps.append(p.astype(_BF16))
        l_ref[...] += part
        acc_ref[...] += jnp.dot(jnp.concatenate(ps, axis=1), v_ref[pl.ds(off, tq), :],
                                preferred_element_type=_F32)
        return carry

    lax.fori_loop(0, qi + 1, weigh, 0)
    lam = _lambda(lq1_ref, lk1_ref, lq2_ref, lk2_ref, lam_init)
    o = acc_ref[...] / jnp.sum(l_ref[...], axis=-1, keepdims=True)
    o = o[:tq] - lam * o[tq:]
    o_ref[...] = _sub_layernorm(o, g_ref, lam_init).astype(o_ref.dtype)


def _attn_prompt(q, k, v, lam_params, subln_g, *, batch, seq, n_heads, dh, lam_init, tq=512):
    hw = 2 * dh
    assert seq % tq == 0 and tq % CHUNK == 0
    nq = seq // tq
    small = pl.BlockSpec((1, dh), lambda b, h, i: (0, 0))
    return pl.pallas_call(
        functools.partial(_attn_prompt_kernel, tq=tq, dh=dh, lam_init=lam_init),
        grid=(batch, n_heads, nq),
        in_specs=[small, small, small, small,
                  pl.BlockSpec((1, hw), lambda b, h, i: (0, 0)),
                  pl.BlockSpec((tq, hw), lambda b, h, i: (b * nq + i, h)),
                  pl.BlockSpec((seq, hw), lambda b, h, i: (b, h)),
                  pl.BlockSpec((seq, hw), lambda b, h, i: (b, h))],
        out_specs=pl.BlockSpec((tq, hw), lambda b, h, i: (b * nq + i, h)),
        out_shape=jax.ShapeDtypeStruct(q.shape, _BF16),
        scratch_shapes=[pltpu.VMEM((2 * tq, seq), _F32), pltpu.VMEM((2 * tq, LANES), _F32),
                        pltpu.VMEM((2 * tq, LANES), _F32), pltpu.VMEM((2 * tq, hw), _F32)],
        compiler_params=_params("parallel", "parallel", "arbitrary"),
        name="attn_prompt",
    )(*lam_params, subln_g.reshape(1, hw), q, k, v)


def _attn_sample_kernel(lq1_ref, lk1_ref, lq2_ref, lk2_ref, g_ref, q_ref, kn_ref, vn_ref, ka_ref, kb_ref,
                        vlo_ref, vhi_ref, o_ref, *, dh, lam_init):
    scale = dh ** -0.5
    past = ka_ref.shape[0]
    hw = 2 * dh
    k_rows = [r.reshape(past * SUBLANES, dh) for r in (ka_ref, kb_ref)]
    v_rows = [r.reshape(past * SUBLANES, dh) for r in (vlo_ref, vhi_ref)]
    lam = _lambda(lq1_ref, lk1_ref, lq2_ref, lk2_ref, lam_init)
    per_block = SUBLANES // 2
    for hl in range(SUBLANES):
        cols = slice(hl * hw, (hl + 1) * hw)
        q = q_ref[:, cols]
        kn = kn_ref[:, cols]
        a_p = a_n = None
        for c in range(2):
            row = (hl % per_block) * 2 + c
            kp = k_rows[hl // per_block][pl.ds(row, past, stride=SUBLANES), :].astype(_BF16)
            qc = q[:, c * dh:(c + 1) * dh]
            s_p = lax.dot_general(qc, kp, _NT, preferred_element_type=_F32) * scale
            s_n = lax.dot_general(qc, kn[:, c * dh:(c + 1) * dh], _NT, preferred_element_type=_F32) * scale
            m = jnp.maximum(jnp.max(s_p, axis=-1, keepdims=True), jnp.max(s_n, axis=-1, keepdims=True))
            e_p = jnp.exp(s_p - m)
            e_n = jnp.exp(s_n - m)
            denom = jnp.sum(e_p, axis=-1, keepdims=True) + jnp.sum(e_n, axis=-1, keepdims=True)
            p_p = e_p / denom
            p_n = e_n / denom
            if c == 0:
                a_p, a_n = p_p, p_n
            else:
                a_p, a_n = a_p - lam * p_p, a_n - lam * p_n
        vp = jnp.concatenate([v[pl.ds(hl, past, stride=SUBLANES), :] for v in v_rows], axis=1).astype(_BF16)
        o = (jnp.dot(a_p.astype(_BF16), vp, preferred_element_type=_F32)
             + jnp.dot(a_n.astype(_BF16), vn_ref[:, cols], preferred_element_type=_F32))
        o_ref[:, cols] = _sub_layernorm(o, g_ref, lam_init).astype(o_ref.dtype)


def _attn_sample(q, k, v, cache_k, cache_v, layer, lam_params, subln_g, *, batch, seq, n_heads, dh, lam_init):
    hw = 2 * dh
    past = cache_k.shape[2]
    assert n_heads % SUBLANES == 0 and dh == LANES
    gw = SUBLANES * hw
    ck = cache_k.reshape(-1, past, n_heads * 2, dh)
    cv = cache_v.reshape(-1, past, n_heads, hw)
    base = layer * batch
    small = pl.BlockSpec((1, dh), lambda b, g: (0, 0))
    new = pl.BlockSpec((seq, gw), lambda b, g: (b, g))
    old = lambda sub, lane: pl.BlockSpec((None, past, SUBLANES, dh), lambda b, g: (base + b, 0, sub(g), lane))
    return pl.pallas_call(
        functools.partial(_attn_sample_kernel, dh=dh, lam_init=lam_init),
        grid=(batch, n_heads // SUBLANES),
        in_specs=[small, small, small, small, pl.BlockSpec((1, hw), lambda b, g: (0, 0)),
                  new, new, new,
                  old(lambda g: 2 * g, 0), old(lambda g: 2 * g + 1, 0), old(lambda g: g, 0), old(lambda g: g, 1)],
        out_specs=new,
        out_shape=jax.ShapeDtypeStruct(q.shape, _BF16),
        compiler_params=_params("parallel", "parallel"),
        name="attn_sample",
    )(*lam_params, subln_g.reshape(1, hw), q, k, v, ck, ck, cv, cv)


def _conv_ln_silu(ext_ref, c_ref, w_ref, bdw_ref, g_ref, b_ref, o_ref, *, rows):
    nslab = c_ref.shape[0]
    d = nslab * LANES
    first = CONV_HALO - (CONV_WIDTH - 1)

    def slab(si, carry):
        lanes = pl.ds(pl.multiple_of(si * LANES, LANES), LANES)
        taps = [jnp.broadcast_to(w_ref[pl.ds(j, 1), lanes], (SUBLANES, LANES)) for j in range(CONV_WIDTH)]
        bias = jnp.broadcast_to(bdw_ref[:, lanes], (SUBLANES, LANES))
        for r0 in range(0, rows, 2 * SUBLANES):
            for parity in range(2):
                acc = bias
                for j in range(CONV_WIDTH):
                    acc = acc + taps[j] * ext_ref[si, pl.ds(first + r0 + parity + j, SUBLANES, stride=2), :]
                c_ref[si, pl.ds(r0 + parity, SUBLANES, stride=2), :] = acc
        return carry

    lax.fori_loop(0, nslab, slab, 0)

    total = c_ref[0]
    for si in range(1, nslab):
        total = total + c_ref[si]
    mu = jnp.broadcast_to(jnp.sum(total, axis=-1, keepdims=True) * (1.0 / d), (rows, LANES))
    total = None
    for si in range(nslab):
        cc = c_ref[si] - mu
        total = cc * cc if total is None else total + cc * cc
    var = jnp.sum(total, axis=-1, keepdims=True) * (1.0 / d)
    inv = jnp.broadcast_to(lax.rsqrt(var + EPS), (rows, LANES))
    for si in range(nslab):
        lanes = slice(si * LANES, (si + 1) * LANES)
        y = (c_ref[si] - mu) * inv * g_ref[:, lanes] + b_ref[:, lanes]
        o_ref[:, lanes] = (y * jax.nn.sigmoid(y)).astype(o_ref.dtype)


def _conv_prompt_kernel(x_ref, halo_ref, w_ref, bdw_ref, g_ref, b_ref, o_ref, ext_ref, c_ref, *, rows):
    i = pl.program_id(1)
    for si in range(ext_ref.shape[0]):
        lanes = slice(si * LANES, (si + 1) * LANES)
        halo = halo_ref[:, lanes]
        ext_ref[si, pl.ds(0, CONV_HALO), :] = jnp.where(i == 0, jnp.zeros_like(halo), halo)
        ext_ref[si, pl.ds(CONV_HALO, rows), :] = x_ref[:, lanes]
    _conv_ln_silu(ext_ref, c_ref, w_ref, bdw_ref, g_ref, b_ref, o_ref, rows=rows)


def _conv_sample_kernel(x_ref, st_ref, w_ref, bdw_ref, g_ref, b_ref, o_ref, ext_ref, c_ref, *, rows):
    ctx = CONV_WIDTH - 1
    for si in range(ext_ref.shape[0]):
        lanes = slice(si * LANES, (si + 1) * LANES)
        ext_ref[si, pl.ds(CONV_HALO - ctx, ctx), :] = st_ref[0, :, lanes]
        ext_ref[si, pl.ds(CONV_HALO, rows), :] = x_ref[:, lanes]
    _conv_ln_silu(ext_ref, c_ref, w_ref, bdw_ref, g_ref, b_ref, o_ref, rows=rows)


def _conv_module(x, state, w_dw, b_dw, ln_g, ln_b, *, batch, seq):
    d = x.shape[1]
    vec = lambda a: a.reshape(1, d)
    if state is None:
        rows = 128
        nt = seq // rows
        grid = (batch, nt)
        per = rows // CONV_HALO
        const = lambda b, i: (0, 0)
        in_specs = [pl.BlockSpec((rows, d), lambda b, i: (b * nt + i, 0)),
                    pl.BlockSpec((CONV_HALO, d), lambda b, i: (jnp.maximum((b * nt + i) * per - 1, 0), 0))]
        out_spec = pl.BlockSpec((rows, d), lambda b, i: (b * nt + i, 0))
        body = functools.partial(_conv_prompt_kernel, rows=rows)
        args = [x, x]
        sem = ("parallel", "arbitrary")
        name = "conv_prompt"
    else:
        rows = seq
        grid = (batch,)
        const = lambda b: (0, 0)
        in_specs = [pl.BlockSpec((rows, d), lambda b: (b, 0)),
                    pl.BlockSpec((1, CONV_WIDTH - 1, d), lambda b: (b, 0, 0))]
        out_spec = pl.BlockSpec((rows, d), lambda b: (b, 0))
        body = functools.partial(_conv_sample_kernel, rows=rows)
        args = [x, state]
        sem = ("parallel",)
        name = "conv_sample"
    in_specs += [pl.BlockSpec((CONV_WIDTH, d), const)] + [pl.BlockSpec((1, d), const)] * 3
    return pl.pallas_call(
        body,
        grid=grid,
        in_specs=in_specs,
        out_specs=out_spec,
        out_shape=jax.ShapeDtypeStruct(x.shape, _BF16),
        scratch_shapes=[pltpu.VMEM((d // LANES, CONV_HALO + rows, LANES), _F32),
                        pltpu.VMEM((d // LANES, rows, LANES), _F32)],
        compiler_params=_params(*sem),
        name=name,
    )(*args, w_dw, vec(b_dw), vec(ln_g), vec(ln_b))


def _top_rows(s, k):
    n = s.shape[0]
    iota = lax.broadcasted_iota(jnp.int32, s.shape, 0).astype(_F32)
    vals, idxs = [], []
    for _ in range(k):
        m = jnp.max(s, axis=0, keepdims=True)
        am = jnp.min(jnp.where(s == m, iota, float(n)), axis=0, keepdims=True)
        vals.append(m)
        idxs.append(am)
        s = jnp.where(iota == am, -jnp.inf, s)
    return jnp.concatenate(vals, axis=0), jnp.concatenate(idxs, axis=0)


def _pick_row(rank, table):
    out = jnp.zeros_like(rank)
    for k in range(table.shape[0]):
        out = jnp.where(rank == float(k), table[k:k + 1], out)
    return out


def _router_kernel(q_ref, keys_ref, a_ref, b_ref, g_ref, *, half):
    kk = PEER_TOPK
    lim = 4
    assert lim * lim >= kk
    q = q_ref[...]
    tops = []
    for c in range(2):
        s = lax.dot_general(keys_ref[0, c], q[:, c * half:(c + 1) * half], _NT, preferred_element_type=_F32)
        tops.append(_top_rows(s, kk))
    (s0, i0), (s1, i1) = tops
    rank = lax.broadcasted_iota(jnp.int32, s0.shape, 0)
    rank_f = rank.astype(_F32)
    never = float(kk * kk)
    cands, flats = [], []
    for k1 in range(lim):
        ok = rank < kk // (k1 + 1)
        cands.append(jnp.where(ok, s0[k1:k1 + 1] + s1, -jnp.inf))
        flats.append(jnp.where(ok, k1 * kk + rank_f, never))
    for k2 in range(lim):
        if kk // (k2 + 1) <= lim:
            continue
        ok = (rank >= lim) & (rank < kk // (k2 + 1))
        cands.append(jnp.where(ok, s0 + s1[k2:k2 + 1], -jnp.inf))
        flats.append(jnp.where(ok, rank_f * kk + k2, never))
    cand = jnp.concatenate(cands, axis=0)
    flat = jnp.concatenate(flats, axis=0)
    best, pos = [], []
    for _ in range(kk):
        m = jnp.max(cand, axis=0, keepdims=True)
        p = jnp.min(jnp.where(cand == m, flat, never), axis=0, keepdims=True)
        best.append(m)
        pos.append(p)
        cand = jnp.where(flat == p, -jnp.inf, cand)
    best = jnp.concatenate(best, axis=0)
    pos = jnp.concatenate(pos, axis=0)
    k1 = jnp.floor(pos * (1.0 / kk))
    k2 = pos - k1 * kk
    e = jnp.exp(best - best[0:1])
    g_ref[...] = e / jnp.sum(e, axis=0, keepdims=True)
    a_ref[...] = _pick_row(k1, i0).astype(jnp.int32)
    b_ref[...] = _pick_row(k2, i1).astype(jnp.int32)


def _peer_route(q, sub_keys, *, tm=256):
    n = q.shape[0]
    heads, _, n_keys, half = sub_keys.shape
    nsel = heads * PEER_TOPK
    out_spec = pl.BlockSpec((PEER_TOPK, tm), lambda i, h: (h, i))
    return pl.pallas_call(
        functools.partial(_router_kernel, half=half),
        grid=(n // tm, heads),
        in_specs=[pl.BlockSpec((tm, 2 * half), lambda i, h: (i, h)),
                  pl.BlockSpec((1, 2, n_keys, half), lambda i, h: (h, 0, 0, 0))],
        out_specs=[out_spec, out_spec, out_spec],
        out_shape=[jax.ShapeDtypeStruct((nsel, n), jnp.int32), jax.ShapeDtypeStruct((nsel, n), jnp.int32),
                   jax.ShapeDtypeStruct((nsel, n), _F32)],
        compiler_params=_params("parallel", "parallel"),
        name="peer_route",
    )(q, sub_keys)


def _gate_matrix_kernel(a_ref, b_ref, g_ref, o_ref, at_ref, bt_ref, gt_ref, s_ref, *, n_keys, tg):
    at_ref[...] = a_ref[...].T
    bt_ref[...] = b_ref[...].T
    gt_ref[...] = g_ref[...].T
    nsel = at_ref.shape[1]
    pitch = _gate_pitch(n_keys)
    key_id = lax.broadcasted_iota(jnp.int32, (n_keys, nsel), 0)

    def token(t, carry):
        row = pl.ds(t, 1)
        lhs = jnp.where(at_ref[row, :] == key_id, gt_ref[row, :], 0.0).astype(_BF16)
        rhs = jnp.where(bt_ref[row, :] == key_id, 1.0, 0.0).astype(_BF16)
        s_ref[pl.ds(pl.multiple_of(t * pitch, SUBLANES), n_keys), :] = lax.dot_general(
            lhs, rhs, _NT, preferred_element_type=_F32)
        return carry

    lax.fori_loop(0, tg, token, 0, unroll=8)
    for i in range(n_keys):
        o_ref[:, i * n_keys:(i + 1) * n_keys] = s_ref[pl.ds(i, tg, stride=pitch), :].astype(o_ref.dtype)


def _gate_pitch(n_keys):
    return n_keys + SUBLANES if (n_keys // SUBLANES) % 2 == 0 else n_keys


def _gate_matrix(sel_a, sel_b, gate, *, n_keys, tg=128):
    nsel, n = sel_a.shape
    in_spec = pl.BlockSpec((nsel, tg), lambda i: (0, i))
    return pl.pallas_call(
        functools.partial(_gate_matrix_kernel, n_keys=n_keys, tg=tg),
        grid=(n // tg,),
        in_specs=[in_spec, in_spec, in_spec],
        out_specs=pl.BlockSpec((tg, n_keys * n_keys), lambda i: (i, 0)),
        out_shape=jax.ShapeDtypeStruct((n, n_keys * n_keys), _BF16),
        scratch_shapes=[pltpu.VMEM((tg, nsel), jnp.int32), pltpu.VMEM((tg, nsel), jnp.int32),
                        pltpu.VMEM((tg, nsel), _F32), pltpu.VMEM((tg * _gate_pitch(n_keys), n_keys), _F32)],
        compiler_params=_params("parallel"),
        name="peer_gates",
    )(sel_a, sel_b, gate)


def _peer_weights_kernel(x_ref, u_ref, gates_ref, o_ref):
    act = lax.dot_general(x_ref[...], u_ref[...].astype(x_ref.dtype), _NT, preferred_element_type=_F32)
    gelu = 0.5 * act * (1.0 + lax.erf(act * (2.0 ** -0.5)))
    o_ref[...] = (gelu * gates_ref[...].astype(_F32)).astype(o_ref.dtype)


def _peer_weights(x, u, layer, gates, *, tm=1024, te=512):
    n, d = x.shape
    n_exp = u.shape[1]
    tm = min(tm, n)
    return pl.pallas_call(
        _peer_weights_kernel,
        grid=(n // tm, n_exp // te),
        in_specs=[pl.BlockSpec((tm, d), lambda i, j: (i, 0)),
                  pl.BlockSpec((None, te, d), lambda i, j: (layer, j, 0)),
                  pl.BlockSpec((tm, te), lambda i, j: (i, j))],
        out_specs=pl.BlockSpec((tm, te), lambda i, j: (i, j)),
        out_shape=jax.ShapeDtypeStruct((n, n_exp), _BF16),
        compiler_params=_params("parallel", "parallel"),
        name="peer_weights",
    )(x, u, gates)


def _peer(h, norm_g, w_q, sub_keys, u_tabs, v_tabs, layer):
    assert h.shape[0] % LANES == 0
    n_keys = sub_keys.shape[2]
    hn = _rmsnorm(h, norm_g, _BF16)
    q = _matmul(hn, w_q, out_dtypes=(_BF16,), name="peer_query")
    sel_a, sel_b, gate = _peer_route(q, sub_keys)
    gates = _gate_matrix(sel_a, sel_b, gate, n_keys=n_keys)
    weights = _peer_weights(hn, u_tabs, layer, gates)
    return _matmul(weights, v_tabs, w_layer=layer, resid=h, tiles=(1024, 512, 2048), name="peer_mix")


def kernel(x_prompt, x_sample, cache_k, cache_v, state_conv, mixer_norm_g, ffn_norm_g, final_norm_g, w_qkv, lambda_q1, lambda_k1, lambda_q2, lambda_k2, subln_g, w_o, w_pw1, b_pw1, w_dw, b_dw, conv_ln_g, conv_ln_b, w_pw2, b_pw2, peer_wq, peer_sub_keys, peer_u, peer_v):
    batch, seq, d = x_prompt.shape
    dbatch, dseq, _ = x_sample.shape
    depth = mixer_norm_g.shape[0]
    n_heads, dh = cache_k.shape[3], cache_k.shape[5]
    past = cache_k.shape[2]
    assert dseq == CHUNK and past % CHUNK == 0
    sets = [dict(h=x_prompt.reshape(batch * seq, d), batch=batch, seq=seq, prompt=True),
            dict(h=x_sample.reshape(dbatch * dseq, d), batch=dbatch, seq=dseq, prompt=False)]
    new_k, new_v, new_conv = [[], []], [[], []], [[], []]

    for i in range(depth):
        if i % 2 == 0:
            a = i // 2
            lam_init = 0.8 - 0.6 * math.exp(-0.3 * i)
            wq, wk, wv = (_layer_bf16(w_qkv, a, s * d, d) for s in range(3))
            wo = _layer_bf16(w_o, a)
            lam_params = [p[a].reshape(1, dh) for p in (lambda_q1, lambda_k1, lambda_q2, lambda_k2)]
            for si, st in enumerate(sets):
                hn = _rmsnorm(st["h"], mixer_norm_g[i], _BF16)
                q = _matmul(hn, wq, out_dtypes=(_BF16,), name="q_proj")
                k32, k16 = _matmul(hn, wk, out_dtypes=(_F32, _BF16), lane_rows_first=True,
                                   tiles=(512, SUBLANES * LANES, d), name="k_proj")
                v32, v16 = _matmul(hn, wv, out_dtypes=(_F32, _BF16), name="v_proj")
                kw = dict(batch=st["batch"], seq=st["seq"], n_heads=n_heads, dh=dh, lam_init=lam_init)
                if st["prompt"]:
                    o = _attn_prompt(q, k16, v16, lam_params, subln_g[a], **kw)
                else:
                    o = _attn_sample(q, k16, v16, cache_k, cache_v, a, lam_params, subln_g[a], **kw)
                st["h"] = _matmul(o, wo, resid=st["h"], name="o_proj")
                new_k[si].append(k32.reshape(st["batch"], st["seq"], n_heads, 2, dh))
                new_v[si].append(v32.reshape(st["batch"], st["seq"], n_heads, 2 * dh))
        else:
            c = i // 2
            w1, w2 = _layer_bf16(w_pw1, c), _layer_bf16(w_pw2, c)
            for si, st in enumerate(sets):
                hn = _rmsnorm(st["h"], mixer_norm_g[i], _BF16)
                glu = _glu_matmul(hn, w1, b_pw1[c])
                state = None if st["prompt"] else state_conv[c]
                z = _conv_module(glu, state, w_dw[c], b_dw[c], conv_ln_g[c], conv_ln_b[c],
                                 batch=st["batch"], seq=st["seq"])
                st["h"] = _matmul(z, w2, bias=b_pw2[c], resid=st["h"], name="pw2")
                glu3 = glu.reshape(st["batch"], st["seq"], d)
                if st["prompt"]:
                    new_conv[si].append(glu3[:, seq - (CONV_WIDTH - 1):])
                else:
                    ctx = jnp.concatenate([state, glu3], axis=1)
                    new_conv[si].append(ctx[:, -(CONV_WIDTH - 1):])
        wpq = _layer_bf16(peer_wq, i)
        keys = peer_sub_keys[i].astype(_BF16)
        for st in sets:
            st["h"] = _peer(st["h"], ffn_norm_g[i], wpq, keys, peer_u, peer_v, i)

    y_prompt = _rmsnorm(sets[0]["h"], final_norm_g, _F32).reshape(batch, seq, d)
    y_sample = _rmsnorm(sets[1]["h"], final_norm_g, _F32).reshape(dbatch, dseq, d)
    return (y_prompt, y_sample, jnp.stack(new_k[0]), jnp.stack(new_v[0]), jnp.stack(new_conv[0]),
            jnp.stack(new_k[1]), jnp.stack(new_v[1]), jnp.stack(new_conv[1]))
```

```python
import functools
import math

import jax
import jax.numpy as jnp
from jax import lax
from jax.experimental import pallas as pl
from jax.experimental.pallas import tpu as pltpu

EPS = 1e-6
CHUNK = 64
CONV_WIDTH = 31
CONV_HALO = 32
PEER_TOPK = 16
LANES = 128
SUBLANES = 8
VMEM_LIMIT = 56 * 1024 * 1024

_NT = (((1,), (1,)), ((), ()))
_F32 = jnp.float32
_BF16 = jnp.bfloat16


def _params(*sem):
    return pltpu.CompilerParams(dimension_semantics=sem, vmem_limit_bytes=VMEM_LIMIT)


def _cast_kernel(x_ref, o_ref):
    o_ref[...] = x_ref[0].astype(o_ref.dtype)


def _layer_bf16(w, layer, col0=0, ncols=None):
    _, rows, cols = w.shape
    ncols = cols if ncols is None else ncols
    tr, tc = min(512, rows), min(2048, ncols)
    assert col0 % tc == 0
    c0 = col0 // tc
    return pl.pallas_call(
        _cast_kernel,
        grid=(rows // tr, ncols // tc),
        in_specs=[pl.BlockSpec((1, tr, tc), lambda i, j: (layer, i, j + c0))],
        out_specs=pl.BlockSpec((tr, tc), lambda i, j: (i, j)),
        out_shape=jax.ShapeDtypeStruct((rows, ncols), _BF16),
        compiler_params=_params("parallel", "parallel"),
        name="to_bf16",
    )(w)


def _rmsnorm_kernel(x_ref, g_ref, o_ref):
    x = x_ref[...]
    ms = jnp.mean(x * x, axis=-1, keepdims=True)
    o_ref[...] = (x * lax.rsqrt(ms + EPS) * g_ref[...]).astype(o_ref.dtype)


def _rmsnorm(x, g, out_dtype, tr=256):
    m, d = x.shape
    return pl.pallas_call(
        _rmsnorm_kernel,
        grid=(m // tr,),
        in_specs=[pl.BlockSpec((tr, d), lambda i: (i, 0)), pl.BlockSpec((1, d), lambda i: (0, 0))],
        out_specs=pl.BlockSpec((tr, d), lambda i: (i, 0)),
        out_shape=jax.ShapeDtypeStruct((m, d), out_dtype),
        compiler_params=_params("parallel"),
        name="rmsnorm",
    )(x, g.reshape(1, d))


def _mm_kernel(*refs, has_bias, has_resid, n_out, multi_k):
    x_ref, w_ref = refs[0], refs[1]
    pos = 2
    b_ref = r_ref = None
    if has_bias:
        b_ref = refs[pos]
        pos += 1
    if has_resid:
        r_ref = refs[pos]
        pos += 1
    out_refs = refs[pos:pos + n_out]

    def finish(r):
        if has_bias:
            r = r + b_ref[...]
        if has_resid:
            r = r + r_ref[...]
        for o in out_refs:
            if len(o.shape) == 2:
                o[...] = r.astype(o.dtype)
            else:
                rows, groups, _ = o.shape
                flat = o.reshape(rows * groups, LANES)
                for c in range(groups):
                    flat[pl.ds(c, rows, stride=groups), :] = r[:, c * LANES:(c + 1) * LANES].astype(o.dtype)

    def product():
        return jnp.dot(x_ref[...], w_ref[...], preferred_element_type=_F32)

    if not multi_k:
        finish(product())
        return
    acc_ref = refs[pos + n_out]
    k = pl.program_id(2)

    @pl.when(k == 0)
    def _():
        acc_ref[...] = product()

    @pl.when(k > 0)
    def _():
        acc_ref[...] += product()

    @pl.when(k == pl.num_programs(2) - 1)
    def _():
        finish(acc_ref[...])


def _mm_tiles(m, n, k):
    return min(1024, m), min(512, n), min(4096, k)


def _matmul(x, w, *, bias=None, resid=None, out_dtypes=(_F32,), lane_rows_first=False, tiles=None, name="matmul"):
    m, kdim = x.shape
    n = w.shape[-1]
    tm, tn, tk = tiles or _mm_tiles(m, n, kdim)
    tm, tn, tk = min(tm, m), min(tn, n), min(tk, kdim)
    multi_k = kdim > tk
    out_specs = [pl.BlockSpec((tm, tn), lambda i, j, k: (i, j)) for _ in out_dtypes]
    out_shape = [jax.ShapeDtypeStruct((m, n), dt) for dt in out_dtypes]
    if lane_rows_first:
        assert (tn // LANES) % SUBLANES == 0 or tn == n
        out_specs[0] = pl.BlockSpec((tm, tn // LANES, LANES), lambda i, j, k: (i, j, 0))
        out_shape[0] = jax.ShapeDtypeStruct((m, n // LANES, LANES), out_dtypes[0])
    in_specs = [pl.BlockSpec((tm, tk), lambda i, j, k: (i, k)),
                pl.BlockSpec((tk, tn), lambda i, j, k: (k, j))]
    args = [x, w]
    if bias is not None:
        in_specs.append(pl.BlockSpec((1, tn), lambda i, j, k: (0, j)))
        args.append(bias.reshape(1, n))
    if resid is not None:
        in_specs.append(pl.BlockSpec((tm, tn), lambda i, j, k: (i, j)))
        args.append(resid)
    outs = pl.pallas_call(
        functools.partial(_mm_kernel, has_bias=bias is not None, has_resid=resid is not None,
                          n_out=len(out_dtypes), multi_k=multi_k),
        grid=(m // tm, n // tn, kdim // tk),
        in_specs=in_specs,
        out_specs=out_specs,
        out_shape=out_shape,
        scratch_shapes=[pltpu.VMEM((tm, tn), _F32)] if multi_k else [],
        compiler_params=_params("parallel", "parallel", "arbitrary"),
        name=name,
    )(*args)
    return outs if len(out_dtypes) > 1 else outs[0]


def _glu_kernel(x_ref, wa_ref, wg_ref, ba_ref, bg_ref, o_ref):
    x = x_ref[...]
    a = jnp.dot(x, wa_ref[...], preferred_element_type=_F32) + ba_ref[...]
    gate = jnp.dot(x, wg_ref[...], preferred_element_type=_F32) + bg_ref[...]
    o_ref[...] = a * jax.nn.sigmoid(gate)


def _glu_matmul(x, w, b):
    m, kdim = x.shape
    n = w.shape[1] // 2
    tm, tn, tk = _mm_tiles(m, n, kdim)
    assert tk == kdim
    nj = n // tn
    b2 = b.reshape(1, 2 * n)
    return pl.pallas_call(
        _glu_kernel,
        grid=(m // tm, nj),
        in_specs=[pl.BlockSpec((tm, kdim), lambda i, j: (i, 0)),
                  pl.BlockSpec((kdim, tn), lambda i, j: (0, j)),
                  pl.BlockSpec((kdim, tn), lambda i, j: (0, j + nj)),
                  pl.BlockSpec((1, tn), lambda i, j: (0, j)),
                  pl.BlockSpec((1, tn), lambda i, j: (0, j + nj))],
        out_specs=pl.BlockSpec((tm, tn), lambda i, j: (i, j)),
        out_shape=jax.ShapeDtypeStruct((m, n), _F32),
        compiler_params=_params("parallel", "parallel"),
        name="glu_matmul",
    )(x, w, w, b2, b2)


def _lambda(lq1_ref, lk1_ref, lq2_ref, lk2_ref, lam_init):
    s1 = jnp.sum(lq1_ref[...] * lk1_ref[...], axis=-1, keepdims=True)
    s2 = jnp.sum(lq2_ref[...] * lk2_ref[...], axis=-1, keepdims=True)
    return jnp.exp(s1) - jnp.exp(s2) + lam_init


def _sub_layernorm(o, g_ref, lam_init):
    ms = jnp.mean(o * o, axis=-1, keepdims=True)
    return (o * lax.rsqrt(ms + EPS)) * g_ref[...] * (1.0 - lam_init)


def _attn_prompt_kernel(lq1_ref, lk1_ref, lq2_ref, lk2_ref, g_ref, q_ref, k_ref, v_ref, o_ref,
                        s_ref, mx_ref, l_ref, acc_ref, *, tq, dh, lam_init):
    qi = pl.program_id(2)
    scale = dh ** -0.5 * math.log2(math.e)
    q = q_ref[...]
    mx_ref[...] = jnp.full_like(mx_ref, -jnp.inf)
    l_ref[...] = jnp.zeros_like(l_ref)
    acc_ref[...] = jnp.zeros_like(acc_ref)

    def scores(kj, mask):
        off = pl.multiple_of(kj * tq, tq)
        k = k_ref[pl.ds(off, tq), :]
        for c in range(2):
            s = lax.dot_general(q[:, c * dh:(c + 1) * dh], k[:, c * dh:(c + 1) * dh], _NT,
                                preferred_element_type=_F32) * scale
            if mask is not None:
                s = jnp.where(mask, s, -jnp.inf)
            rows = pl.ds(c * tq, tq)
            s_ref[rows, pl.ds(off, tq)] = s
            part = s[:, :LANES]
            for j in range(1, tq // LANES):
                part = jnp.maximum(part, s[:, j * LANES:(j + 1) * LANES])
            mx_ref[rows, :] = jnp.maximum(mx_ref[rows, :], part)

    def scores_body(kj, carry):
        scores(kj, None)
        return carry

    lax.fori_loop(0, qi, scores_body, 0)
    row_chunk = lax.broadcasted_iota(jnp.int32, (tq, tq), 0) // CHUNK
    col_chunk = lax.broadcasted_iota(jnp.int32, (tq, tq), 1) // CHUNK
    scores(qi, col_chunk <= row_chunk)

    m = jnp.broadcast_to(jnp.max(mx_ref[...], axis=-1, keepdims=True), mx_ref.shape)
    mx_ref[...] = m

    def weigh(kj, carry):
        off = pl.multiple_of(kj * tq, tq)
        row_max = mx_ref[...]
        ps = []
        part = None
        for j in range(tq // LANES):
            p = jnp.exp2(s_ref[:, pl.ds(pl.multiple_of(off + j * LANES, LANES), LANES)] - row_max)
            part = p if part is None else part + p
            ps.append(p.astype(_BF16))
        l_ref[...] += part
        acc_ref[...] += jnp.dot(jnp.concatenate(ps, axis=1), v_ref[pl.ds(off, tq), :],
                                preferred_element_type=_F32)
        return carry

    lax.fori_loop(0, qi + 1, weigh, 0)
    lam = _lambda(lq1_ref, lk1_ref, lq2_ref, lk2_ref, lam_init)
    o = acc_ref[...] / jnp.sum(l_ref[...], axis=-1, keepdims=True)
    o = o[:tq] - lam * o[tq:]
    o_ref[...] = _sub_layernorm(o, g_ref, lam_init).astype(o_ref.dtype)


def _attn_prompt(q, k, v, lam_params, subln_g, *, batch, seq, n_heads, dh, lam_init, tq=512):
    hw = 2 * dh
    assert seq % tq == 0 and tq % CHUNK == 0
    nq = seq // tq
    small = pl.BlockSpec((1, dh), lambda b, h, i: (0, 0))
    return pl.pallas_call(
        functools.partial(_attn_prompt_kernel, tq=tq, dh=dh, lam_init=lam_init),
        grid=(batch, n_heads, nq),
        in_specs=[small, small, small, small,
                  pl.BlockSpec((1, hw), lambda b, h, i: (0, 0)),
                  pl.BlockSpec((tq, hw), lambda b, h, i: (b * nq + i, h)),
                  pl.BlockSpec((seq, hw), lambda b, h, i: (b, h)),
                  pl.BlockSpec((seq, hw), lambda b, h, i: (b, h))],
        out_specs=pl.BlockSpec((tq, hw), lambda b, h, i: (b * nq + i, h)),
        out_shape=jax.ShapeDtypeStruct(q.shape, _BF16),
        scratch_shapes=[pltpu.VMEM((2 * tq, seq), _F32), pltpu.VMEM((2 * tq, LANES), _F32),
                        pltpu.VMEM((2 * tq, LANES), _F32), pltpu.VMEM((2 * tq, hw), _F32)],
        compiler_params=_params("parallel", "parallel", "arbitrary"),
        name="attn_prompt",
    )(*lam_params, subln_g.reshape(1, hw), q, k, v)


def _attn_sample_kernel(lq1_ref, lk1_ref, lq2_ref, lk2_ref, g_ref, q_ref, kn_ref, vn_ref, ka_ref, kb_ref,
                        vlo_ref, vhi_ref, o_ref, *, dh, lam_init):
    scale = dh ** -0.5
    past = ka_ref.shape[0]
    hw = 2 * dh
    k_rows = [r.reshape(past * SUBLANES, dh) for r in (ka_ref, kb_ref)]
    v_rows = [r.reshape(past * SUBLANES, dh) for r in (vlo_ref, vhi_ref)]
    lam = _lambda(lq1_ref, lk1_ref, lq2_ref, lk2_ref, lam_init)
    per_block = SUBLANES // 2
    for hl in range(SUBLANES):
        cols = slice(hl * hw, (hl + 1) * hw)
        q = q_ref[:, cols]
        kn = kn_ref[:, cols]
        a_p = a_n = None
        for c in range(2):
            row = (hl % per_block) * 2 + c
            kp = k_rows[hl // per_block][pl.ds(row, past, stride=SUBLANES), :].astype(_BF16)
            qc = q[:, c * dh:(c + 1) * dh]
            s_p = lax.dot_general(qc, kp, _NT, preferred_element_type=_F32) * scale
            s_n = lax.dot_general(qc, kn[:, c * dh:(c + 1) * dh], _NT, preferred_element_type=_F32) * scale
            m = jnp.maximum(jnp.max(s_p, axis=-1, keepdims=True), jnp.max(s_n, axis=-1, keepdims=True))
            e_p = jnp.exp(s_p - m)
            e_n = jnp.exp(s_n - m)
            denom = jnp.sum(e_p, axis=-1, keepdims=True) + jnp.sum(e_n, axis=-1, keepdims=True)
            p_p = e_p / denom
            p_n = e_n / denom
            if c == 0:
                a_p, a_n = p_p, p_n
            else:
                a_p, a_n = a_p - lam * p_p, a_n - lam * p_n
        vp = jnp.concatenate([v[pl.ds(hl, past, stride=SUBLANES), :] for v in v_rows], axis=1).astype(_BF16)
        o = (jnp.dot(a_p.astype(_BF16), vp, preferred_element_type=_F32)
             + jnp.dot(a_n.astype(_BF16), vn_ref[:, cols], preferred_element_type=_F32))
        o_ref[:, cols] = _sub_layernorm(o, g_ref, lam_init).astype(o_ref.dtype)


def _attn_sample(q, k, v, cache_k, cache_v, layer, lam_params, subln_g, *, batch, seq, n_heads, dh, lam_init):
    hw = 2 * dh
    past = cache_k.shape[2]
    assert n_heads % SUBLANES == 0 and dh == LANES
    gw = SUBLANES * hw
    ck = cache_k.reshape(-1, past, n_heads * 2, dh)
    cv = cache_v.reshape(-1, past, n_heads, hw)
    base = layer * batch
    small = pl.BlockSpec((1, dh), lambda b, g: (0, 0))
    new = pl.BlockSpec((seq, gw), lambda b, g: (b, g))
    old = lambda sub, lane: pl.BlockSpec((None, past, SUBLANES, dh), lambda b, g: (base + b, 0, sub(g), lane))
    return pl.pallas_call(
        functools.partial(_attn_sample_kernel, dh=dh, lam_init=lam_init),
        grid=(batch, n_heads // SUBLANES),
        in_specs=[small, small, small, small, pl.BlockSpec((1, hw), lambda b, g: (0, 0)),
                  new, new, new,
                  old(lambda g: 2 * g, 0), old(lambda g: 2 * g + 1, 0), old(lambda g: g, 0), old(lambda g: g, 1)],
        out_specs=new,
        out_shape=jax.ShapeDtypeStruct(q.shape, _BF16),
        compiler_params=_params("parallel", "parallel"),
        name="attn_sample",
    )(*lam_params, subln_g.reshape(1, hw), q, k, v, ck, ck, cv, cv)


def _conv_ln_silu(ext_ref, c_ref, w_ref, bdw_ref, g_ref, b_ref, o_ref, *, rows):
    nslab = c_ref.shape[0]
    d = nslab * LANES
    first = CONV_HALO - (CONV_WIDTH - 1)

    def slab(si, carry):
        lanes = pl.ds(pl.multiple_of(si * LANES, LANES), LANES)
        taps = [jnp.broadcast_to(w_ref[pl.ds(j, 1), lanes], (SUBLANES, LANES)) for j in range(CONV_WIDTH)]
        bias = jnp.broadcast_to(bdw_ref[:, lanes], (SUBLANES, LANES))
        for r0 in range(0, rows, 2 * SUBLANES):
            for parity in range(2):
                acc = bias
                for j in range(CONV_WIDTH):
                    acc = acc + taps[j] * ext_ref[si, pl.ds(first + r0 + parity + j, SUBLANES, stride=2), :]
                c_ref[si, pl.ds(r0 + parity, SUBLANES, stride=2), :] = acc
        return carry

    lax.fori_loop(0, nslab, slab, 0)

    total = c_ref[0]
    for si in range(1, nslab):
        total = total + c_ref[si]
    mu = jnp.broadcast_to(jnp.sum(total, axis=-1, keepdims=True) * (1.0 / d), (rows, LANES))
    total = None
    for si in range(nslab):
        cc = c_ref[si] - mu
        total = cc * cc if total is None else total + cc * cc
    var = jnp.sum(total, axis=-1, keepdims=True) * (1.0 / d)
    inv = jnp.broadcast_to(lax.rsqrt(var + EPS), (rows, LANES))
    for si in range(nslab):
        lanes = slice(si * LANES, (si + 1) * LANES)
        y = (c_ref[si] - mu) * inv * g_ref[:, lanes] + b_ref[:, lanes]
        o_ref[:, lanes] = (y * jax.nn.sigmoid(y)).astype(o_ref.dtype)


def _conv_prompt_kernel(x_ref, halo_ref, w_ref, bdw_ref, g_ref, b_ref, o_ref, ext_ref, c_ref, *, rows):
    i = pl.program_id(1)
    for si in range(ext_ref.shape[0]):
        lanes = slice(si * LANES, (si + 1) * LANES)
        halo = halo_ref[:, lanes]
        ext_ref[si, pl.ds(0, CONV_HALO), :] = jnp.where(i == 0, jnp.zeros_like(halo), halo)
        ext_ref[si, pl.ds(CONV_HALO, rows), :] = x_ref[:, lanes]
    _conv_ln_silu(ext_ref, c_ref, w_ref, bdw_ref, g_ref, b_ref, o_ref, rows=rows)


def _conv_sample_kernel(x_ref, st_ref, w_ref, bdw_ref, g_ref, b_ref, o_ref, ext_ref, c_ref, *, rows):
    ctx = CONV_WIDTH - 1
    for si in range(ext_ref.shape[0]):
        lanes = slice(si * LANES, (si + 1) * LANES)
        ext_ref[si, pl.ds(CONV_HALO - ctx, ctx), :] = st_ref[0, :, lanes]
        ext_ref[si, pl.ds(CONV_HALO, rows), :] = x_ref[:, lanes]
    _conv_ln_silu(ext_ref, c_ref, w_ref, bdw_ref, g_ref, b_ref, o_ref, rows=rows)


def _conv_module(x, state, w_dw, b_dw, ln_g, ln_b, *, batch, seq):
    d = x.shape[1]
    vec = lambda a: a.reshape(1, d)
    if state is None:
        rows = 128
        nt = seq // rows
        grid = (batch, nt)
        per = rows // CONV_HALO
        const = lambda b, i: (0, 0)
        in_specs = [pl.BlockSpec((rows, d), lambda b, i: (b * nt + i, 0)),
                    pl.BlockSpec((CONV_HALO, d), lambda b, i: (jnp.maximum((b * nt + i) * per - 1, 0), 0))]
        out_spec = pl.BlockSpec((rows, d), lambda b, i: (b * nt + i, 0))
        body = functools.partial(_conv_prompt_kernel, rows=rows)
        args = [x, x]
        sem = ("parallel", "arbitrary")
        name = "conv_prompt"
    else:
        rows = seq
        grid = (batch,)
        const = lambda b: (0, 0)
        in_specs = [pl.BlockSpec((rows, d), lambda b: (b, 0)),
                    pl.BlockSpec((1, CONV_WIDTH - 1, d), lambda b: (b, 0, 0))]
        out_spec = pl.BlockSpec((rows, d), lambda b: (b, 0))
        body = functools.partial(_conv_sample_kernel, rows=rows)
        args = [x, state]
        sem = ("parallel",)
        name = "conv_sample"
    in_specs += [pl.BlockSpec((CONV_WIDTH, d), const)] + [pl.BlockSpec((1, d), const)] * 3
    return pl.pallas_call(
        body,
        grid=grid,
        in_specs=in_specs,
        out_specs=out_spec,
        out_shape=jax.ShapeDtypeStruct(x.shape, _BF16),
        scratch_shapes=[pltpu.VMEM((d // LANES, CONV_HALO + rows, LANES), _F32),
                        pltpu.VMEM((d // LANES, rows, LANES), _F32)],
        compiler_params=_params(*sem),
        name=name,
    )(*args, w_dw, vec(b_dw), vec(ln_g), vec(ln_b))


def _top_rows(s, k):
    n = s.shape[0]
    iota = lax.broadcasted_iota(jnp.int32, s.shape, 0).astype(_F32)
    vals, idxs = [], []
    for _ in range(k):
        m = jnp.max(s, axis=0, keepdims=True)
        am = jnp.min(jnp.where(s == m, iota, float(n)), axis=0, keepdims=True)
        vals.append(m)
        idxs.append(am)
        s = jnp.where(iota == am, -jnp.inf, s)
    return jnp.concatenate(vals, axis=0), jnp.concatenate(idxs, axis=0)


def _pick_row(rank, table):
    out = jnp.zeros_like(rank)
    for k in range(table.shape[0]):
        out = jnp.where(rank == float(k), table[k:k + 1], out)
    return out


def _router_kernel(q_ref, keys_ref, a_ref, b_ref, g_ref, *, half):
    kk = PEER_TOPK
    lim = 4
    assert lim * lim >= kk
    q = q_ref[...]
    tops = []
    for c in range(2):
        s = lax.dot_general(keys_ref[0, c], q[:, c * half:(c + 1) * half], _NT, preferred_element_type=_F32)
        tops.append(_top_rows(s, kk))
    (s0, i0), (s1, i1) = tops
    rank = lax.broadcasted_iota(jnp.int32, s0.shape, 0)
    rank_f = rank.astype(_F32)
    never = float(kk * kk)
    cands, flats = [], []
    for k1 in range(lim):
        ok = rank < kk // (k1 + 1)
        cands.append(jnp.where(ok, s0[k1:k1 + 1] + s1, -jnp.inf))
        flats.append(jnp.where(ok, k1 * kk + rank_f, never))
    for k2 in range(lim):
        if kk // (k2 + 1) <= lim:
            continue
        ok = (rank >= lim) & (rank < kk // (k2 + 1))
        cands.append(jnp.where(ok, s0 + s1[k2:k2 + 1], -jnp.inf))
        flats.append(jnp.where(ok, rank_f * kk + k2, never))
    cand = jnp.concatenate(cands, axis=0)
    flat = jnp.concatenate(flats, axis=0)
    best, pos = [], []
    for _ in range(kk):
        m = jnp.max(cand, axis=0, keepdims=True)
        p = jnp.min(jnp.where(cand == m, flat, never), axis=0, keepdims=True)
        best.append(m)
        pos.append(p)
        cand = jnp.where(flat == p, -jnp.inf, cand)
    best = jnp.concatenate(best, axis=0)
    pos = jnp.concatenate(pos, axis=0)
    k1 = jnp.floor(pos * (1.0 / kk))
    k2 = pos - k1 * kk
    e = jnp.exp(best - best[0:1])
    g_ref[...] = e / jnp.sum(e, axis=0, keepdims=True)
    a_ref[...] = _pick_row(k1, i0).astype(jnp.int32)
    b_ref[...] = _pick_row(k2, i1).astype(jnp.int32)


def _peer_route(q, sub_keys, *, tm=1024):
    n = q.shape[0]
    tm = min(tm, n)
    heads, _, n_keys, half = sub_keys.shape
    nsel = heads * PEER_TOPK
    out_spec = pl.BlockSpec((PEER_TOPK, tm), lambda i, h: (h, i))
    return pl.pallas_call(
        functools.partial(_router_kernel, half=half),
        grid=(n // tm, heads),
        in_specs=[pl.BlockSpec((tm, 2 * half), lambda i, h: (i, h)),
                  pl.BlockSpec((1, 2, n_keys, half), lambda i, h: (h, 0, 0, 0))],
        out_specs=[out_spec, out_spec, out_spec],
        out_shape=[jax.ShapeDtypeStruct((nsel, n), jnp.int32), jax.ShapeDtypeStruct((nsel, n), jnp.int32),
                   jax.ShapeDtypeStruct((nsel, n), _F32)],
        compiler_params=_params("parallel", "parallel"),
        name="peer_route",
    )(q, sub_keys)


def _gate_matrix_kernel(a_ref, b_ref, g_ref, o_ref, at_ref, bt_ref, gt_ref, s_ref, *, n_keys, tg):
    at_ref[...] = a_ref[...].T
    bt_ref[...] = b_ref[...].T
    gt_ref[...] = g_ref[...].T
    nsel = at_ref.shape[1]
    pitch = _gate_pitch(n_keys)
    key_id = lax.broadcasted_iota(jnp.int32, (n_keys, nsel), 0)

    def token(t, carry):
        row = pl.ds(t, 1)
        lhs = jnp.where(at_ref[row, :] == key_id, gt_ref[row, :], 0.0).astype(_BF16)
        rhs = jnp.where(bt_ref[row, :] == key_id, 1.0, 0.0).astype(_BF16)
        s_ref[pl.ds(pl.multiple_of(t * pitch, SUBLANES), n_keys), :] = lax.dot_general(
            lhs, rhs, _NT, preferred_element_type=_F32)
        return carry

    lax.fori_loop(0, tg, token, 0, unroll=8)
    for i in range(n_keys):
        o_ref[:, i * n_keys:(i + 1) * n_keys] = s_ref[pl.ds(i, tg, stride=pitch), :].astype(o_ref.dtype)


def _gate_pitch(n_keys):
    return n_keys + SUBLANES if (n_keys // SUBLANES) % 2 == 0 else n_keys


def _gate_matrix(sel_a, sel_b, gate, *, n_keys, tg=128):
    nsel, n = sel_a.shape
    in_spec = pl.BlockSpec((nsel, tg), lambda i: (0, i))
    return pl.pallas_call(
        functools.partial(_gate_matrix_kernel, n_keys=n_keys, tg=tg),
        grid=(n // tg,),
        in_specs=[in_spec, in_spec, in_spec],
        out_specs=pl.BlockSpec((tg, n_keys * n_keys), lambda i: (i, 0)),
        out_shape=jax.ShapeDtypeStruct((n, n_keys * n_keys), _BF16),
        scratch_shapes=[pltpu.VMEM((tg, nsel), jnp.int32), pltpu.VMEM((tg, nsel), jnp.int32),
                        pltpu.VMEM((tg, nsel), _F32), pltpu.VMEM((tg * _gate_pitch(n_keys), n_keys), _F32)],
        compiler_params=_params("parallel"),
        name="peer_gates",
    )(sel_a, sel_b, gate)


def _peer_weights_kernel(x_ref, u_ref, gates_ref, o_ref):
    act = lax.dot_general(x_ref[...], u_ref[...].astype(x_ref.dtype), _NT, preferred_element_type=_F32)
    gelu = 0.5 * act * (1.0 + lax.erf(act * (2.0 ** -0.5)))
    o_ref[...] = (gelu * gates_ref[...].astype(_F32)).astype(o_ref.dtype)


def _peer_weights(x, u, layer, gates, *, tm=1024, te=512):
    n, d = x.shape
    n_exp = u.shape[1]
    tm = min(tm, n)
    return pl.pallas_call(
        _peer_weights_kernel,
        grid=(n // tm, n_exp // te),
        in_specs=[pl.BlockSpec((tm, d), lambda i, j: (i, 0)),
                  pl.BlockSpec((None, te, d), lambda i, j: (layer, j, 0)),
                  pl.BlockSpec((tm, te), lambda i, j: (i, j))],
        out_specs=pl.BlockSpec((tm, te), lambda i, j: (i, j)),
        out_shape=jax.ShapeDtypeStruct((n, n_exp), _BF16),
        compiler_params=_params("parallel", "parallel"),
        name="peer_weights",
    )(x, u, gates)


def _peer(h, norm_g, w_q, sub_keys, u_tabs, v_tab, layer):
    assert h.shape[0] % LANES == 0
    n_keys = sub_keys.shape[2]
    hn = _rmsnorm(h, norm_g, _BF16)
    q = _matmul(hn, w_q, out_dtypes=(_BF16,), name="peer_query")
    sel_a, sel_b, gate = _peer_route(q, sub_keys)
    gates = _gate_matrix(sel_a, sel_b, gate, n_keys=n_keys)
    weights = _peer_weights(hn, u_tabs, layer, gates)
    return _matmul(weights, v_tab, resid=h, tiles=(1024, 1024, 2048), name="peer_mix")


def kernel(x_prompt, x_sample, cache_k, cache_v, state_conv, mixer_norm_g, ffn_norm_g, final_norm_g, w_qkv, lambda_q1, lambda_k1, lambda_q2, lambda_k2, subln_g, w_o, w_pw1, b_pw1, w_dw, b_dw, conv_ln_g, conv_ln_b, w_pw2, b_pw2, peer_wq, peer_sub_keys, peer_u, peer_v):
    batch, seq, d = x_prompt.shape
    dbatch, dseq, _ = x_sample.shape
    depth = mixer_norm_g.shape[0]
    n_heads, dh = cache_k.shape[3], cache_k.shape[5]
    past = cache_k.shape[2]
    assert dseq == CHUNK and past % CHUNK == 0
    sets = [dict(h=x_prompt.reshape(batch * seq, d), batch=batch, seq=seq, prompt=True),
            dict(h=x_sample.reshape(dbatch * dseq, d), batch=dbatch, seq=dseq, prompt=False)]
    new_k, new_v, new_conv = [[], []], [[], []], [[], []]

    for i in range(depth):
        if i % 2 == 0:
            a = i // 2
            lam_init = 0.8 - 0.6 * math.exp(-0.3 * i)
            wq, wk, wv = (_layer_bf16(w_qkv, a, s * d, d) for s in range(3))
            wo = _layer_bf16(w_o, a)
            lam_params = [p[a].reshape(1, dh) for p in (lambda_q1, lambda_k1, lambda_q2, lambda_k2)]
            for si, st in enumerate(sets):
                hn = _rmsnorm(st["h"], mixer_norm_g[i], _BF16)
                q = _matmul(hn, wq, out_dtypes=(_BF16,), name="q_proj")
                k32, k16 = _matmul(hn, wk, out_dtypes=(_F32, _BF16), lane_rows_first=True,
                                   tiles=(512, SUBLANES * LANES, d), name="k_proj")
                v32, v16 = _matmul(hn, wv, out_dtypes=(_F32, _BF16), name="v_proj")
                kw = dict(batch=st["batch"], seq=st["seq"], n_heads=n_heads, dh=dh, lam_init=lam_init)
                if st["prompt"]:
                    o = _attn_prompt(q, k16, v16, lam_params, subln_g[a], **kw)
                else:
                    o = _attn_sample(q, k16, v16, cache_k, cache_v, a, lam_params, subln_g[a], **kw)
                st["h"] = _matmul(o, wo, resid=st["h"], name="o_proj")
                new_k[si].append(k32.reshape(st["batch"], st["seq"], n_heads, 2, dh))
                new_v[si].append(v32.reshape(st["batch"], st["seq"], n_heads, 2 * dh))
        else:
            c = i // 2
            w1, w2 = _layer_bf16(w_pw1, c), _layer_bf16(w_pw2, c)
            for si, st in enumerate(sets):
                hn = _rmsnorm(st["h"], mixer_norm_g[i], _BF16)
                glu = _glu_matmul(hn, w1, b_pw1[c])
                state = None if st["prompt"] else state_conv[c]
                z = _conv_module(glu, state, w_dw[c], b_dw[c], conv_ln_g[c], conv_ln_b[c],
                                 batch=st["batch"], seq=st["seq"])
                st["h"] = _matmul(z, w2, bias=b_pw2[c], resid=st["h"], name="pw2")
                glu3 = glu.reshape(st["batch"], st["seq"], d)
                if st["prompt"]:
                    new_conv[si].append(glu3[:, seq - (CONV_WIDTH - 1):])
                else:
                    ctx = jnp.concatenate([state, glu3], axis=1)
                    new_conv[si].append(ctx[:, -(CONV_WIDTH - 1):])
        wpq, v_tab = _layer_bf16(peer_wq, i), _layer_bf16(peer_v, i)
        keys = peer_sub_keys[i].astype(_BF16)
        for st in sets:
            st["h"] = _peer(st["h"], ffn_norm_g[i], wpq, keys, peer_u, v_tab, i)

    y_prompt = _rmsnorm(sets[0]["h"], final_norm_g, _F32).reshape(batch, seq, d)
    y_sample = _rmsnorm(sets[1]["h"], final_norm_g, _F32).reshape(dbatch, dseq, d)
    return (y_prompt, y_sample, jnp.stack(new_k[0]), jnp.stack(new_v[0]), jnp.stack(new_conv[0]),
            jnp.stack(new_k[1]), jnp.stack(new_v[1]), jnp.stack(new_conv[1]))
```

```python
import functools
import math

import jax
import jax.numpy as jnp
from jax import lax
from jax.experimental import pallas as pl
from jax.experimental.pallas import tpu as pltpu

EPS = 1e-6
CHUNK = 64
CONV_WIDTH = 31
CONV_HALO = 32
PEER_TOPK = 16
LANES = 128
SUBLANES = 8
VMEM_LIMIT = 56 * 1024 * 1024

_NT = (((1,), (1,)), ((), ()))
_F32 = jnp.float32
_BF16 = jnp.bfloat16


def _params(*sem):
    return pltpu.CompilerParams(dimension_semantics=sem, vmem_limit_bytes=VMEM_LIMIT)


def _cast_kernel(x_ref, o_ref):
    o_ref[...] = x_ref[0].astype(o_ref.dtype)


def _layer_bf16(w, layer, col0=0, ncols=None):
    _, rows, cols = w.shape
    ncols = cols if ncols is None else ncols
    tr, tc = min(512, rows), min(2048, ncols)
    assert col0 % tc == 0
    c0 = col0 // tc
    return pl.pallas_call(
        _cast_kernel,
        grid=(rows // tr, ncols // tc),
        in_specs=[pl.BlockSpec((1, tr, tc), lambda i, j: (layer, i, j + c0))],
        out_specs=pl.BlockSpec((tr, tc), lambda i, j: (i, j)),
        out_shape=jax.ShapeDtypeStruct((rows, ncols), _BF16),
        compiler_params=_params("parallel", "parallel"),
        name="to_bf16",
    )(w)


def _rmsnorm_kernel(x_ref, g_ref, o_ref):
    x = x_ref[...]
    ms = jnp.mean(x * x, axis=-1, keepdims=True)
    o_ref[...] = (x * lax.rsqrt(ms + EPS) * g_ref[...]).astype(o_ref.dtype)


def _rmsnorm(x, g, out_dtype, tr=256):
    m, d = x.shape
    return pl.pallas_call(
        _rmsnorm_kernel,
        grid=(m // tr,),
        in_specs=[pl.BlockSpec((tr, d), lambda i: (i, 0)), pl.BlockSpec((1, d), lambda i: (0, 0))],
        out_specs=pl.BlockSpec((tr, d), lambda i: (i, 0)),
        out_shape=jax.ShapeDtypeStruct((m, d), out_dtype),
        compiler_params=_params("parallel"),
        name="rmsnorm",
    )(x, g.reshape(1, d))


def _mm_kernel(*refs, has_bias, has_resid, n_out, multi_k):
    x_ref, w_ref = refs[0], refs[1]
    pos = 2
    b_ref = r_ref = None
    if has_bias:
        b_ref = refs[pos]
        pos += 1
    if has_resid:
        r_ref = refs[pos]
        pos += 1
    out_refs = refs[pos:pos + n_out]

    def finish(r):
        if has_bias:
            r = r + b_ref[...]
        if has_resid:
            r = r + r_ref[...]
        for o in out_refs:
            if len(o.shape) == 2:
                o[...] = r.astype(o.dtype)
            else:
                rows, groups, _ = o.shape
                flat = o.reshape(rows * groups, LANES)
                for c in range(groups):
                    flat[pl.ds(c, rows, stride=groups), :] = r[:, c * LANES:(c + 1) * LANES].astype(o.dtype)

    def product():
        return jnp.dot(x_ref[...], w_ref[...], preferred_element_type=_F32)

    if not multi_k:
        finish(product())
        return
    acc_ref = refs[pos + n_out]
    k = pl.program_id(2)

    @pl.when(k == 0)
    def _():
        acc_ref[...] = product()

    @pl.when(k > 0)
    def _():
        acc_ref[...] += product()

    @pl.when(k == pl.num_programs(2) - 1)
    def _():
        finish(acc_ref[...])


def _mm_tiles(m, n, k):
    return min(1024, m), min(512, n), min(4096, k)


def _matmul(x, w, *, bias=None, resid=None, out_dtypes=(_F32,), lane_rows_first=False, tiles=None, name="matmul"):
    m, kdim = x.shape
    n = w.shape[-1]
    tm, tn, tk = tiles or _mm_tiles(m, n, kdim)
    tm, tn, tk = min(tm, m), min(tn, n), min(tk, kdim)
    multi_k = kdim > tk
    out_specs = [pl.BlockSpec((tm, tn), lambda i, j, k: (i, j)) for _ in out_dtypes]
    out_shape = [jax.ShapeDtypeStruct((m, n), dt) for dt in out_dtypes]
    if lane_rows_first:
        assert (tn // LANES) % SUBLANES == 0 or tn == n
        out_specs[0] = pl.BlockSpec((tm, tn // LANES, LANES), lambda i, j, k: (i, j, 0))
        out_shape[0] = jax.ShapeDtypeStruct((m, n // LANES, LANES), out_dtypes[0])
    in_specs = [pl.BlockSpec((tm, tk), lambda i, j, k: (i, k)),
                pl.BlockSpec((tk, tn), lambda i, j, k: (k, j))]
    args = [x, w]
    if bias is not None:
        in_specs.append(pl.BlockSpec((1, tn), lambda i, j, k: (0, j)))
        args.append(bias.reshape(1, n))
    if resid is not None:
        in_specs.append(pl.BlockSpec((tm, tn), lambda i, j, k: (i, j)))
        args.append(resid)
    outs = pl.pallas_call(
        functools.partial(_mm_kernel, has_bias=bias is not None, has_resid=resid is not None,
                          n_out=len(out_dtypes), multi_k=multi_k),
        grid=(m // tm, n // tn, kdim // tk),
        in_specs=in_specs,
        out_specs=out_specs,
        out_shape=out_shape,
        scratch_shapes=[pltpu.VMEM((tm, tn), _F32)] if multi_k else [],
        compiler_params=_params("parallel", "parallel", "arbitrary"),
        name=name,
    )(*args)
    return outs if len(out_dtypes) > 1 else outs[0]


def _glu_kernel(x_ref, wa_ref, wg_ref, ba_ref, bg_ref, o_ref):
    x = x_ref[...]
    a = jnp.dot(x, wa_ref[...], preferred_element_type=_F32) + ba_ref[...]
    gate = jnp.dot(x, wg_ref[...], preferred_element_type=_F32) + bg_ref[...]
    o_ref[...] = a * jax.nn.sigmoid(gate)


def _glu_matmul(x, w, b):
    m, kdim = x.shape
    n = w.shape[1] // 2
    tm, tn, tk = _mm_tiles(m, n, kdim)
    assert tk == kdim
    nj = n // tn
    b2 = b.reshape(1, 2 * n)
    return pl.pallas_call(
        _glu_kernel,
        grid=(m // tm, nj),
        in_specs=[pl.BlockSpec((tm, kdim), lambda i, j: (i, 0)),
                  pl.BlockSpec((kdim, tn), lambda i, j: (0, j)),
                  pl.BlockSpec((kdim, tn), lambda i, j: (0, j + nj)),
                  pl.BlockSpec((1, tn), lambda i, j: (0, j)),
                  pl.BlockSpec((1, tn), lambda i, j: (0, j + nj))],
        out_specs=pl.BlockSpec((tm, tn), lambda i, j: (i, j)),
        out_shape=jax.ShapeDtypeStruct((m, n), _F32),
        compiler_params=_params("parallel", "parallel"),
        name="glu_matmul",
    )(x, w, w, b2, b2)


def _lambda(lq1_ref, lk1_ref, lq2_ref, lk2_ref, lam_init):
    s1 = jnp.sum(lq1_ref[...] * lk1_ref[...], axis=-1, keepdims=True)
    s2 = jnp.sum(lq2_ref[...] * lk2_ref[...], axis=-1, keepdims=True)
    return jnp.exp(s1) - jnp.exp(s2) + lam_init


def _sub_layernorm(o, g_ref, lam_init):
    ms = jnp.mean(o * o, axis=-1, keepdims=True)
    return (o * lax.rsqrt(ms + EPS)) * g_ref[...] * (1.0 - lam_init)


def _attn_prompt_kernel(lq1_ref, lk1_ref, lq2_ref, lk2_ref, g_ref, q_ref, k_ref, v_ref, o_ref,
                        s_ref, mx_ref, l_ref, acc_ref, *, tq, dh, lam_init):
    qi = pl.program_id(2)
    scale = dh ** -0.5 * math.log2(math.e)
    q = q_ref[...]
    mx_ref[...] = jnp.full_like(mx_ref, -jnp.inf)
    l_ref[...] = jnp.zeros_like(l_ref)
    acc_ref[...] = jnp.zeros_like(acc_ref)

    def scores(kj, mask):
        off = pl.multiple_of(kj * tq, tq)
        k = k_ref[pl.ds(off, tq), :]
        for c in range(2):
            s = lax.dot_general(q[:, c * dh:(c + 1) * dh], k[:, c * dh:(c + 1) * dh], _NT,
                                preferred_element_type=_F32) * scale
            if mask is not None:
                s = jnp.where(mask, s, -jnp.inf)
            rows = pl.ds(c * tq, tq)
            s_ref[rows, pl.ds(off, tq)] = s
            part = s[:, :LANES]
            for j in range(1, tq // LANES):
                part = jnp.maximum(part, s[:, j * LANES:(j + 1) * LANES])
            mx_ref[rows, :] = jnp.maximum(mx_ref[rows, :], part)

    def scores_body(kj, carry):
        scores(kj, None)
        return carry

    lax.fori_loop(0, qi, scores_body, 0)
    row_chunk = lax.broadcasted_iota(jnp.int32, (tq, tq), 0) // CHUNK
    col_chunk = lax.broadcasted_iota(jnp.int32, (tq, tq), 1) // CHUNK
    scores(qi, col_chunk <= row_chunk)

    m = jnp.broadcast_to(jnp.max(mx_ref[...], axis=-1, keepdims=True), mx_ref.shape)
    mx_ref[...] = m

    def weigh(kj, carry):
        off = pl.multiple_of(kj * tq, tq)
        row_max = mx_ref[...]
        ps = []
        part = None
        for j in range(tq // LANES):
            p = jnp.exp2(s_ref[:, pl.ds(pl.multiple_of(off + j * LANES, LANES), LANES)] - row_max)
            part = p if part is None else part + p
            ps.append(p.astype(_BF16))
        l_ref[...] += part
        acc_ref[...] += jnp.dot(jnp.concatenate(ps, axis=1), v_ref[pl.ds(off, tq), :],
                                preferred_element_type=_F32)
        return carry

    lax.fori_loop(0, qi + 1, weigh, 0)
    lam = _lambda(lq1_ref, lk1_ref, lq2_ref, lk2_ref, lam_init)
    o = acc_ref[...] / jnp.sum(l_ref[...], axis=-1, keepdims=True)
    o = o[:tq] - lam * o[tq:]
    o_ref[...] = _sub_layernorm(o, g_ref, lam_init).astype(o_ref.dtype)


def _attn_prompt(q, k, v, lam_params, subln_g, *, batch, seq, n_heads, dh, lam_init, tq=512):
    hw = 2 * dh
    assert seq % tq == 0 and tq % CHUNK == 0
    nq = seq // tq
    small = pl.BlockSpec((1, dh), lambda b, h, i: (0, 0))
    return pl.pallas_call(
        functools.partial(_attn_prompt_kernel, tq=tq, dh=dh, lam_init=lam_init),
        grid=(batch, n_heads, nq),
        in_specs=[small, small, small, small,
                  pl.BlockSpec((1, hw), lambda b, h, i: (0, 0)),
                  pl.BlockSpec((tq, hw), lambda b, h, i: (b * nq + i, h)),
                  pl.BlockSpec((seq, hw), lambda b, h, i: (b, h)),
                  pl.BlockSpec((seq, hw), lambda b, h, i: (b, h))],
        out_specs=pl.BlockSpec((tq, hw), lambda b, h, i: (b * nq + i, h)),
        out_shape=jax.ShapeDtypeStruct(q.shape, _BF16),
        scratch_shapes=[pltpu.VMEM((2 * tq, seq), _F32), pltpu.VMEM((2 * tq, LANES), _F32),
                        pltpu.VMEM((2 * tq, LANES), _F32), pltpu.VMEM((2 * tq, hw), _F32)],
        compiler_params=_params("parallel", "parallel", "arbitrary"),
        name="attn_prompt",
    )(*lam_params, subln_g.reshape(1, hw), q, k, v)


def _attn_sample_kernel(lq1_ref, lk1_ref, lq2_ref, lk2_ref, g_ref, q_ref, kn_ref, vn_ref, ka_ref, kb_ref,
                        vlo_ref, vhi_ref, o_ref, *, dh, lam_init):
    scale = dh ** -0.5
    past = ka_ref.shape[0]
    hw = 2 * dh
    k_rows = [r.reshape(past * SUBLANES, dh) for r in (ka_ref, kb_ref)]
    v_rows = [r.reshape(past * SUBLANES, dh) for r in (vlo_ref, vhi_ref)]
    lam = _lambda(lq1_ref, lk1_ref, lq2_ref, lk2_ref, lam_init)
    per_block = SUBLANES // 2
    for hl in range(SUBLANES):
        cols = slice(hl * hw, (hl + 1) * hw)
        q = q_ref[:, cols]
        kn = kn_ref[:, cols]
        a_p = a_n = None
        for c in range(2):
            row = (hl % per_block) * 2 + c
            kp = k_rows[hl // per_block][pl.ds(row, past, stride=SUBLANES), :].astype(_BF16)
            qc = q[:, c * dh:(c + 1) * dh]
            s_p = lax.dot_general(qc, kp, _NT, preferred_element_type=_F32) * scale
            s_n = lax.dot_general(qc, kn[:, c * dh:(c + 1) * dh], _NT, preferred_element_type=_F32) * scale
            m = jnp.maximum(jnp.max(s_p, axis=-1, keepdims=True), jnp.max(s_n, axis=-1, keepdims=True))
            e_p = jnp.exp(s_p - m)
            e_n = jnp.exp(s_n - m)
            denom = jnp.sum(e_p, axis=-1, keepdims=True) + jnp.sum(e_n, axis=-1, keepdims=True)
            p_p = e_p / denom
            p_n = e_n / denom
            if c == 0:
                a_p, a_n = p_p, p_n
            else:
                a_p, a_n = a_p - lam * p_p, a_n - lam * p_n
        vp = jnp.concatenate([v[pl.ds(hl, past, stride=SUBLANES), :] for v in v_rows], axis=1).astype(_BF16)
        o = (jnp.dot(a_p.astype(_BF16), vp, preferred_element_type=_F32)
             + jnp.dot(a_n.astype(_BF16), vn_ref[:, cols], preferred_element_type=_F32))
        o_ref[:, cols] = _sub_layernorm(o, g_ref, lam_init).astype(o_ref.dtype)


def _attn_sample(q, k, v, cache_k, cache_v, layer, lam_params, subln_g, *, batch, seq, n_heads, dh, lam_init):
    hw = 2 * dh
    past = cache_k.shape[2]
    assert n_heads % SUBLANES == 0 and dh == LANES
    gw = SUBLANES * hw
    ck = cache_k.reshape(-1, past, n_heads * 2, dh)
    cv = cache_v.reshape(-1, past, n_heads, hw)
    base = layer * batch
    small = pl.BlockSpec((1, dh), lambda b, g: (0, 0))
    new = pl.BlockSpec((seq, gw), lambda b, g: (b, g))
    old = lambda sub, lane: pl.BlockSpec((None, past, SUBLANES, dh), lambda b, g: (base + b, 0, sub(g), lane))
    return pl.pallas_call(
        functools.partial(_attn_sample_kernel, dh=dh, lam_init=lam_init),
        grid=(batch, n_heads // SUBLANES),
        in_specs=[small, small, small, small, pl.BlockSpec((1, hw), lambda b, g: (0, 0)),
                  new, new, new,
                  old(lambda g: 2 * g, 0), old(lambda g: 2 * g + 1, 0), old(lambda g: g, 0), old(lambda g: g, 1)],
        out_specs=new,
        out_shape=jax.ShapeDtypeStruct(q.shape, _BF16),
        compiler_params=_params("parallel", "parallel"),
        name="attn_sample",
    )(*lam_params, subln_g.reshape(1, hw), q, k, v, ck, ck, cv, cv)


def _conv_ln_silu(ext_ref, c_ref, w_ref, bdw_ref, g_ref, b_ref, o_ref, *, rows):
    nslab = c_ref.shape[0]
    d = nslab * LANES
    first = CONV_HALO - (CONV_WIDTH - 1)

    def slab(si, carry):
        lanes = pl.ds(pl.multiple_of(si * LANES, LANES), LANES)
        taps = [jnp.broadcast_to(w_ref[pl.ds(j, 1), lanes], (SUBLANES, LANES)) for j in range(CONV_WIDTH)]
        bias = jnp.broadcast_to(bdw_ref[:, lanes], (SUBLANES, LANES))
        for r0 in range(0, rows, 2 * SUBLANES):
            for parity in range(2):
                acc = bias
                for j in range(CONV_WIDTH):
                    acc = acc + taps[j] * ext_ref[si, pl.ds(first + r0 + parity + j, SUBLANES, stride=2), :]
                c_ref[si, pl.ds(r0 + parity, SUBLANES, stride=2), :] = acc
        return carry

    lax.fori_loop(0, nslab, slab, 0)

    total = c_ref[0]
    for si in range(1, nslab):
        total = total + c_ref[si]
    mu = jnp.broadcast_to(jnp.sum(total, axis=-1, keepdims=True) * (1.0 / d), (rows, LANES))
    total = None
    for si in range(nslab):
        cc = c_ref[si] - mu
        total = cc * cc if total is None else total + cc * cc
    var = jnp.sum(total, axis=-1, keepdims=True) * (1.0 / d)
    inv = jnp.broadcast_to(lax.rsqrt(var + EPS), (rows, LANES))
    for si in range(nslab):
        lanes = slice(si * LANES, (si + 1) * LANES)
        y = (c_ref[si] - mu) * inv * g_ref[:, lanes] + b_ref[:, lanes]
        o_ref[:, lanes] = (y * jax.nn.sigmoid(y)).astype(o_ref.dtype)


def _conv_prompt_kernel(x_ref, halo_ref, w_ref, bdw_ref, g_ref, b_ref, o_ref, ext_ref, c_ref, *, rows):
    i = pl.program_id(1)
    for si in range(ext_ref.shape[0]):
        lanes = slice(si * LANES, (si + 1) * LANES)
        halo = halo_ref[:, lanes]
        ext_ref[si, pl.ds(0, CONV_HALO), :] = jnp.where(i == 0, jnp.zeros_like(halo), halo)
        ext_ref[si, pl.ds(CONV_HALO, rows), :] = x_ref[:, lanes]
    _conv_ln_silu(ext_ref, c_ref, w_ref, bdw_ref, g_ref, b_ref, o_ref, rows=rows)


def _conv_sample_kernel(x_ref, st_ref, w_ref, bdw_ref, g_ref, b_ref, o_ref, ext_ref, c_ref, *, rows):
    ctx = CONV_WIDTH - 1
    for si in range(ext_ref.shape[0]):
        lanes = slice(si * LANES, (si + 1) * LANES)
        ext_ref[si, pl.ds(CONV_HALO - ctx, ctx), :] = st_ref[0, :, lanes]
        ext_ref[si, pl.ds(CONV_HALO, rows), :] = x_ref[:, lanes]
    _conv_ln_silu(ext_ref, c_ref, w_ref, bdw_ref, g_ref, b_ref, o_ref, rows=rows)


def _conv_module(x, state, w_dw, b_dw, ln_g, ln_b, *, batch, seq):
    d = x.shape[1]
    vec = lambda a: a.reshape(1, d)
    if state is None:
        rows = 128
        nt = seq // rows
        grid = (batch, nt)
        per = rows // CONV_HALO
        const = lambda b, i: (0, 0)
        in_specs = [pl.BlockSpec((rows, d), lambda b, i: (b * nt + i, 0)),
                    pl.BlockSpec((CONV_HALO, d), lambda b, i: (jnp.maximum((b * nt + i) * per - 1, 0), 0))]
        out_spec = pl.BlockSpec((rows, d), lambda b, i: (b * nt + i, 0))
        body = functools.partial(_conv_prompt_kernel, rows=rows)
        args = [x, x]
        sem = ("parallel", "arbitrary")
        name = "conv_prompt"
    else:
        rows = seq
        grid = (batch,)
        const = lambda b: (0, 0)
        in_specs = [pl.BlockSpec((rows, d), lambda b: (b, 0)),
                    pl.BlockSpec((1, CONV_WIDTH - 1, d), lambda b: (b, 0, 0))]
        out_spec = pl.BlockSpec((rows, d), lambda b: (b, 0))
        body = functools.partial(_conv_sample_kernel, rows=rows)
        args = [x, state]
        sem = ("parallel",)
        name = "conv_sample"
    in_specs += [pl.BlockSpec((CONV_WIDTH, d), const)] + [pl.BlockSpec((1, d), const)] * 3
    return pl.pallas_call(
        body,
        grid=grid,
        in_specs=in_specs,
        out_specs=out_spec,
        out_shape=jax.ShapeDtypeStruct(x.shape, _BF16),
        scratch_shapes=[pltpu.VMEM((d // LANES, CONV_HALO + rows, LANES), _F32),
                        pltpu.VMEM((d // LANES, rows, LANES), _F32)],
        compiler_params=_params(*sem),
        name=name,
    )(*args, w_dw, vec(b_dw), vec(ln_g), vec(ln_b))


def _top_rows(s, k):
    n = s.shape[0]
    iota = lax.broadcasted_iota(jnp.int32, s.shape, 0).astype(_F32)
    vals, idxs = [], []
    for _ in range(k):
        m = jnp.max(s, axis=0, keepdims=True)
        am = jnp.min(jnp.where(s == m, iota, float(n)), axis=0, keepdims=True)
        vals.append(m)
        idxs.append(am)
        s = jnp.where(iota == am, -jnp.inf, s)
    return jnp.concatenate(vals, axis=0), jnp.concatenate(idxs, axis=0)


def _pick_row(rank, table):
    out = jnp.zeros_like(rank)
    for k in range(table.shape[0]):
        out = jnp.where(rank == float(k), table[k:k + 1], out)
    return out


def _router_kernel(q_ref, keys_ref, a_ref, b_ref, g_ref, *, half):
    kk = PEER_TOPK
    lim = 4
    assert lim * lim >= kk
    q = q_ref[...]
    tops = []
    for c in range(2):
        s = lax.dot_general(keys_ref[0, c], q[:, c * half:(c + 1) * half], _NT, preferred_element_type=_F32)
        tops.append(_top_rows(s, kk))
    (s0, i0), (s1, i1) = tops
    rank = lax.broadcasted_iota(jnp.int32, s0.shape, 0)
    rank_f = rank.astype(_F32)
    never = float(kk * kk)
    cands, flats = [], []
    for k1 in range(lim):
        ok = rank < kk // (k1 + 1)
        cands.append(jnp.where(ok, s0[k1:k1 + 1] + s1, -jnp.inf))
        flats.append(jnp.where(ok, k1 * kk + rank_f, never))
    for k2 in range(lim):
        if kk // (k2 + 1) <= lim:
            continue
        ok = (rank >= lim) & (rank < kk // (k2 + 1))
        cands.append(jnp.where(ok, s0 + s1[k2:k2 + 1], -jnp.inf))
        flats.append(jnp.where(ok, rank_f * kk + k2, never))
    cand = jnp.concatenate(cands, axis=0)
    flat = jnp.concatenate(flats, axis=0)
    best, pos = [], []
    for _ in range(kk):
        m = jnp.max(cand, axis=0, keepdims=True)
        p = jnp.min(jnp.where(cand == m, flat, never), axis=0, keepdims=True)
        best.append(m)
        pos.append(p)
        cand = jnp.where(flat == p, -jnp.inf, cand)
    best = jnp.concatenate(best, axis=0)
    pos = jnp.concatenate(pos, axis=0)
    k1 = jnp.floor(pos * (1.0 / kk))
    k2 = pos - k1 * kk
    e = jnp.exp(best - best[0:1])
    g_ref[...] = e / jnp.sum(e, axis=0, keepdims=True)
    a_ref[...] = _pick_row(k1, i0).astype(jnp.int32)
    b_ref[...] = _pick_row(k2, i1).astype(jnp.int32)


def _peer_route(q, sub_keys, *, tm=1024):
    n = q.shape[0]
    tm = min(tm, n)
    heads, _, n_keys, half = sub_keys.shape
    nsel = heads * PEER_TOPK
    out_spec = pl.BlockSpec((PEER_TOPK, tm), lambda i, h: (h, i))
    return pl.pallas_call(
        functools.partial(_router_kernel, half=half),
        grid=(n // tm, heads),
        in_specs=[pl.BlockSpec((tm, 2 * half), lambda i, h: (i, h)),
                  pl.BlockSpec((1, 2, n_keys, half), lambda i, h: (h, 0, 0, 0))],
        out_specs=[out_spec, out_spec, out_spec],
        out_shape=[jax.ShapeDtypeStruct((nsel, n), jnp.int32), jax.ShapeDtypeStruct((nsel, n), jnp.int32),
                   jax.ShapeDtypeStruct((nsel, n), _F32)],
        compiler_params=_params("parallel", "parallel"),
        name="peer_route",
    )(q, sub_keys)


def _gate_matrix_kernel(a_ref, b_ref, g_ref, o_ref, at_ref, bt_ref, gt_ref, s_ref, *, n_keys, tg):
    at_ref[...] = a_ref[...].T
    bt_ref[...] = b_ref[...].T
    gt_ref[...] = g_ref[...].T
    nsel = at_ref.shape[1]
    pitch = _gate_pitch(n_keys)
    key_id = lax.broadcasted_iota(jnp.int32, (n_keys, nsel), 0)

    def token(t, carry):
        row = pl.ds(t, 1)
        lhs = jnp.where(at_ref[row, :] == key_id, gt_ref[row, :], 0.0).astype(_BF16)
        rhs = jnp.where(bt_ref[row, :] == key_id, 1.0, 0.0).astype(_BF16)
        s_ref[pl.ds(pl.multiple_of(t * pitch, SUBLANES), n_keys), :] = lax.dot_general(
            lhs, rhs, _NT, preferred_element_type=_F32)
        return carry

    lax.fori_loop(0, tg, token, 0, unroll=32)
    for i in range(n_keys):
        o_ref[:, i * n_keys:(i + 1) * n_keys] = s_ref[pl.ds(i, tg, stride=pitch), :].astype(o_ref.dtype)


def _gate_pitch(n_keys):
    return n_keys + SUBLANES if (n_keys // SUBLANES) % 2 == 0 else n_keys


def _gate_matrix(sel_a, sel_b, gate, *, n_keys, tg=128):
    nsel, n = sel_a.shape
    in_spec = pl.BlockSpec((nsel, tg), lambda i: (0, i))
    return pl.pallas_call(
        functools.partial(_gate_matrix_kernel, n_keys=n_keys, tg=tg),
        grid=(n // tg,),
        in_specs=[in_spec, in_spec, in_spec],
        out_specs=pl.BlockSpec((tg, n_keys * n_keys), lambda i: (i, 0)),
        out_shape=jax.ShapeDtypeStruct((n, n_keys * n_keys), _BF16),
        scratch_shapes=[pltpu.VMEM((tg, nsel), jnp.int32), pltpu.VMEM((tg, nsel), jnp.int32),
                        pltpu.VMEM((tg, nsel), _F32), pltpu.VMEM((tg * _gate_pitch(n_keys), n_keys), _F32)],
        compiler_params=_params("parallel"),
        name="peer_gates",
    )(sel_a, sel_b, gate)


def _peer_weights_kernel(x_ref, u_ref, gates_ref, *rest):
    n_side = (len(rest) - 1) // 2
    o_ref = rest[n_side]
    act = lax.dot_general(x_ref[...], u_ref[...].astype(x_ref.dtype), _NT, preferred_element_type=_F32)
    gelu = 0.5 * act * (1.0 + lax.erf(act * (2.0 ** -0.5)))
    o_ref[...] = (gelu * gates_ref[...].astype(_F32)).astype(o_ref.dtype)
    for src, dst in zip(rest[:n_side], rest[n_side + 1:]):
        dst[...] = src[...].astype(dst.dtype)


def _peer_weights(x, u, layer, gates, side=(), *, tm=1024, te=512):
    n, d = x.shape
    n_exp = u.shape[1]
    tm = min(tm, n)
    grid = (n // tm, n_exp // te)
    steps = grid[0] * grid[1]
    side_in, side_out, side_shape = [], [], []
    for w, wl in side:
        _, rows, cols = w.shape
        chunk = rows // steps
        assert chunk * steps == rows and chunk % (2 * SUBLANES) == 0
        step = lambda i, j: i * grid[1] + j
        side_in.append(pl.BlockSpec((None, chunk, cols), lambda i, j, wl=wl: (wl, step(i, j), 0)))
        side_out.append(pl.BlockSpec((chunk, cols), lambda i, j: (step(i, j), 0)))
        side_shape.append(jax.ShapeDtypeStruct((rows, cols), _BF16))
    outs = pl.pallas_call(
        _peer_weights_kernel,
        grid=grid,
        in_specs=[pl.BlockSpec((tm, d), lambda i, j: (i, 0)),
                  pl.BlockSpec((None, te, d), lambda i, j: (layer, j, 0)),
                  pl.BlockSpec((tm, te), lambda i, j: (i, j))] + side_in,
        out_specs=[pl.BlockSpec((tm, te), lambda i, j: (i, j))] + side_out,
        out_shape=[jax.ShapeDtypeStruct((n, n_exp), _BF16)] + side_shape,
        compiler_params=_params("arbitrary", "arbitrary"),
        name="peer_weights",
    )(x, u, gates, *[w for w, _ in side])
    return outs[0], list(outs[1:])


def _peer_route_gates(h, norm_g, w_q, sub_keys):
    assert h.shape[0] % LANES == 0
    hn = _rmsnorm(h, norm_g, _BF16)
    q = _matmul(hn, w_q, out_dtypes=(_BF16,), name="peer_query")
    sel_a, sel_b, gate = _peer_route(q, sub_keys)
    return hn, _gate_matrix(sel_a, sel_b, gate, n_keys=sub_keys.shape[2])


def _peer_mix(weights, v_tab, h):
    return _matmul(weights, v_tab, resid=h, tiles=(1024, 1024, 2048), name="peer_mix")


def kernel(x_prompt, x_sample, cache_k, cache_v, state_conv, mixer_norm_g, ffn_norm_g, final_norm_g, w_qkv, lambda_q1, lambda_k1, lambda_q2, lambda_k2, subln_g, w_o, w_pw1, b_pw1, w_dw, b_dw, conv_ln_g, conv_ln_b, w_pw2, b_pw2, peer_wq, peer_sub_keys, peer_u, peer_v):
    batch, seq, d = x_prompt.shape
    dbatch, dseq, _ = x_sample.shape
    depth = mixer_norm_g.shape[0]
    n_heads, dh = cache_k.shape[3], cache_k.shape[5]
    past = cache_k.shape[2]
    assert dseq == CHUNK and past % CHUNK == 0
    sets = [dict(h=x_prompt.reshape(batch * seq, d), batch=batch, seq=seq, prompt=True),
            dict(h=x_sample.reshape(dbatch * dseq, d), batch=dbatch, seq=dseq, prompt=False)]
    new_k, new_v, new_conv = [[], []], [[], []], [[], []]
    early = {}

    for i in range(depth):
        if i % 2 == 0:
            a = i // 2
            lam_init = 0.8 - 0.6 * math.exp(-0.3 * i)
            wq, wk, wv = (_layer_bf16(w_qkv, a, s * d, d) for s in range(3))
            wo = _layer_bf16(w_o, a)
            lam_params = [p[a].reshape(1, dh) for p in (lambda_q1, lambda_k1, lambda_q2, lambda_k2)]
            for si, st in enumerate(sets):
                hn = _rmsnorm(st["h"], mixer_norm_g[i], _BF16)
                q = _matmul(hn, wq, out_dtypes=(_BF16,), name="q_proj")
                k32, k16 = _matmul(hn, wk, out_dtypes=(_F32, _BF16), lane_rows_first=True,
                                   tiles=(512, SUBLANES * LANES, d), name="k_proj")
                v32, v16 = _matmul(hn, wv, out_dtypes=(_F32, _BF16), name="v_proj")
                kw = dict(batch=st["batch"], seq=st["seq"], n_heads=n_heads, dh=dh, lam_init=lam_init)
                if st["prompt"]:
                    o = _attn_prompt(q, k16, v16, lam_params, subln_g[a], **kw)
                else:
                    o = _attn_sample(q, k16, v16, cache_k, cache_v, a, lam_params, subln_g[a], **kw)
                st["h"] = _matmul(o, wo, resid=st["h"], name="o_proj")
                new_k[si].append(k32.reshape(st["batch"], st["seq"], n_heads, 2, dh))
                new_v[si].append(v32.reshape(st["batch"], st["seq"], n_heads, 2 * dh))
        else:
            c = i // 2
            w1, w2 = early.pop(("w_pw1", c), None), early.pop(("w_pw2", c), None)
            if w1 is None:
                w1, w2 = _layer_bf16(w_pw1, c), _layer_bf16(w_pw2, c)
            for si, st in enumerate(sets):
                hn = _rmsnorm(st["h"], mixer_norm_g[i], _BF16)
                glu = _glu_matmul(hn, w1, b_pw1[c])
                state = None if st["prompt"] else state_conv[c]
                z = _conv_module(glu, state, w_dw[c], b_dw[c], conv_ln_g[c], conv_ln_b[c],
                                 batch=st["batch"], seq=st["seq"])
                st["h"] = _matmul(z, w2, bias=b_pw2[c], resid=st["h"], name="pw2")
                glu3 = glu.reshape(st["batch"], st["seq"], d)
                if st["prompt"]:
                    new_conv[si].append(glu3[:, seq - (CONV_WIDTH - 1):])
                else:
                    ctx = jnp.concatenate([state, glu3], axis=1)
                    new_conv[si].append(ctx[:, -(CONV_WIDTH - 1):])
        wpq = early.pop(("peer_wq", i), None)
        if wpq is None:
            wpq = _layer_bf16(peer_wq, i)
        keys = peer_sub_keys[i].astype(_BF16)
        jobs = [("peer_v", peer_v, i)]
        if i + 1 < depth:
            jobs.append(("peer_wq", peer_wq, i + 1))
            if (i + 1) % 2 == 1:
                jobs += [("w_pw1", w_pw1, (i + 1) // 2), ("w_pw2", w_pw2, (i + 1) // 2)]
        v_tab = None
        for st in sets:
            hn, gates = _peer_route_gates(st["h"], ffn_norm_g[i], wpq, keys)
            side = [(w, wl) for _, w, wl in jobs] if st["prompt"] else []
            weights, copies = _peer_weights(hn, peer_u, i, gates, side)
            if st["prompt"]:
                v_tab = copies[0]
                early.update({(name, wl): cp for (name, _, wl), cp in zip(jobs[1:], copies[1:])})
            st["h"] = _peer_mix(weights, v_tab, st["h"])

    y_prompt = _rmsnorm(sets[0]["h"], final_norm_g, _F32).reshape(batch, seq, d)
    y_sample = _rmsnorm(sets[1]["h"], final_norm_g, _F32).reshape(dbatch, dseq, d)
    return (y_prompt, y_sample, jnp.stack(new_k[0]), jnp.stack(new_v[0]), jnp.stack(new_conv[0]),
            jnp.stack(new_k[1]), jnp.stack(new_v[1]), jnp.stack(new_conv[1]))
```

```python
import functools
import math

import jax
import jax.numpy as jnp
from jax import lax
from jax.experimental import pallas as pl
from jax.experimental.pallas import tpu as pltpu

EPS = 1e-6
CHUNK = 64
CONV_WIDTH = 31
CONV_HALO = 32
PEER_TOPK = 16
LANES = 128
SUBLANES = 8
VMEM_LIMIT = 56 * 1024 * 1024

_NT = (((1,), (1,)), ((), ()))
_F32 = jnp.float32
_BF16 = jnp.bfloat16


def _params(*sem):
    return pltpu.CompilerParams(dimension_semantics=sem, vmem_limit_bytes=VMEM_LIMIT)


def _cast_kernel(x_ref, o_ref):
    o_ref[...] = x_ref[0].astype(o_ref.dtype)


def _layer_bf16(w, layer, col0=0, ncols=None):
    _, rows, cols = w.shape
    ncols = cols if ncols is None else ncols
    tr, tc = min(512, rows), min(2048, ncols)
    assert col0 % tc == 0
    c0 = col0 // tc
    return pl.pallas_call(
        _cast_kernel,
        grid=(rows // tr, ncols // tc),
        in_specs=[pl.BlockSpec((1, tr, tc), lambda i, j: (layer, i, j + c0))],
        out_specs=pl.BlockSpec((tr, tc), lambda i, j: (i, j)),
        out_shape=jax.ShapeDtypeStruct((rows, ncols), _BF16),
        compiler_params=_params("parallel", "parallel"),
        name="to_bf16",
    )(w)


def _rmsnorm_kernel(x_ref, g_ref, o_ref):
    x = x_ref[...]
    ms = jnp.mean(x * x, axis=-1, keepdims=True)
    o_ref[...] = (x * lax.rsqrt(ms + EPS) * g_ref[...]).astype(o_ref.dtype)


def _rmsnorm(x, g, out_dtype, tr=256):
    m, d = x.shape
    return pl.pallas_call(
        _rmsnorm_kernel,
        grid=(m // tr,),
        in_specs=[pl.BlockSpec((tr, d), lambda i: (i, 0)), pl.BlockSpec((1, d), lambda i: (0, 0))],
        out_specs=pl.BlockSpec((tr, d), lambda i: (i, 0)),
        out_shape=jax.ShapeDtypeStruct((m, d), out_dtype),
        compiler_params=_params("parallel"),
        name="rmsnorm",
    )(x, g.reshape(1, d))


def _mm_kernel(*refs, has_bias, has_resid, n_out, multi_k):
    x_ref, w_ref = refs[0], refs[1]
    pos = 2
    b_ref = r_ref = None
    if has_bias:
        b_ref = refs[pos]
        pos += 1
    if has_resid:
        r_ref = refs[pos]
        pos += 1
    out_refs = refs[pos:pos + n_out]

    def finish(r):
        if has_bias:
            r = r + b_ref[...]
        if has_resid:
            r = r + r_ref[...]
        for o in out_refs:
            if len(o.shape) == 2:
                o[...] = r.astype(o.dtype)
            else:
                rows, groups, _ = o.shape
                flat = o.reshape(rows * groups, LANES)
                for c in range(groups):
                    flat[pl.ds(c, rows, stride=groups), :] = r[:, c * LANES:(c + 1) * LANES].astype(o.dtype)

    def product():
        return jnp.dot(x_ref[...], w_ref[...], preferred_element_type=_F32)

    if not multi_k:
        finish(product())
        return
    acc_ref = refs[pos + n_out]
    k = pl.program_id(2)

    @pl.when(k == 0)
    def _():
        acc_ref[...] = product()

    @pl.when(k > 0)
    def _():
        acc_ref[...] += product()

    @pl.when(k == pl.num_programs(2) - 1)
    def _():
        finish(acc_ref[...])


def _mm_tiles(m, n, k):
    return min(1024, m), min(512, n), min(4096, k)


def _matmul(x, w, *, bias=None, resid=None, out_dtypes=(_F32,), lane_rows_first=False, tiles=None, name="matmul"):
    m, kdim = x.shape
    n = w.shape[-1]
    tm, tn, tk = tiles or _mm_tiles(m, n, kdim)
    tm, tn, tk = min(tm, m), min(tn, n), min(tk, kdim)
    multi_k = kdim > tk
    out_specs = [pl.BlockSpec((tm, tn), lambda i, j, k: (i, j)) for _ in out_dtypes]
    out_shape = [jax.ShapeDtypeStruct((m, n), dt) for dt in out_dtypes]
    if lane_rows_first:
        assert (tn // LANES) % SUBLANES == 0 or tn == n
        out_specs[0] = pl.BlockSpec((tm, tn // LANES, LANES), lambda i, j, k: (i, j, 0))
        out_shape[0] = jax.ShapeDtypeStruct((m, n // LANES, LANES), out_dtypes[0])
    in_specs = [pl.BlockSpec((tm, tk), lambda i, j, k: (i, k)),
                pl.BlockSpec((tk, tn), lambda i, j, k: (k, j))]
    args = [x, w]
    if bias is not None:
        in_specs.append(pl.BlockSpec((1, tn), lambda i, j, k: (0, j)))
        args.append(bias.reshape(1, n))
    if resid is not None:
        in_specs.append(pl.BlockSpec((tm, tn), lambda i, j, k: (i, j)))
        args.append(resid)
    outs = pl.pallas_call(
        functools.partial(_mm_kernel, has_bias=bias is not None, has_resid=resid is not None,
                          n_out=len(out_dtypes), multi_k=multi_k),
        grid=(m // tm, n // tn, kdim // tk),
        in_specs=in_specs,
        out_specs=out_specs,
        out_shape=out_shape,
        scratch_shapes=[pltpu.VMEM((tm, tn), _F32)] if multi_k else [],
        compiler_params=_params("parallel", "parallel", "arbitrary"),
        name=name,
    )(*args)
    return outs if len(out_dtypes) > 1 else outs[0]


def _glu_kernel(x_ref, wa_ref, wg_ref, ba_ref, bg_ref, o_ref):
    x = x_ref[...]
    a = jnp.dot(x, wa_ref[...], preferred_element_type=_F32) + ba_ref[...]
    gate = jnp.dot(x, wg_ref[...], preferred_element_type=_F32) + bg_ref[...]
    o_ref[...] = a * jax.nn.sigmoid(gate)


def _glu_matmul(x, w, b):
    m, kdim = x.shape
    n = w.shape[1] // 2
    tm, tn, tk = _mm_tiles(m, n, kdim)
    assert tk == kdim
    nj = n // tn
    b2 = b.reshape(1, 2 * n)
    return pl.pallas_call(
        _glu_kernel,
        grid=(m // tm, nj),
        in_specs=[pl.BlockSpec((tm, kdim), lambda i, j: (i, 0)),
                  pl.BlockSpec((kdim, tn), lambda i, j: (0, j)),
                  pl.BlockSpec((kdim, tn), lambda i, j: (0, j + nj)),
                  pl.BlockSpec((1, tn), lambda i, j: (0, j)),
                  pl.BlockSpec((1, tn), lambda i, j: (0, j + nj))],
        out_specs=pl.BlockSpec((tm, tn), lambda i, j: (i, j)),
        out_shape=jax.ShapeDtypeStruct((m, n), _F32),
        compiler_params=_params("parallel", "parallel"),
        name="glu_matmul",
    )(x, w, w, b2, b2)


def _lambda(lq1_ref, lk1_ref, lq2_ref, lk2_ref, lam_init):
    s1 = jnp.sum(lq1_ref[...] * lk1_ref[...], axis=-1, keepdims=True)
    s2 = jnp.sum(lq2_ref[...] * lk2_ref[...], axis=-1, keepdims=True)
    return jnp.exp(s1) - jnp.exp(s2) + lam_init


def _sub_layernorm(o, g_ref, lam_init):
    ms = jnp.mean(o * o, axis=-1, keepdims=True)
    return (o * lax.rsqrt(ms + EPS)) * g_ref[...] * (1.0 - lam_init)


def _attn_prompt_kernel(lq1_ref, lk1_ref, lq2_ref, lk2_ref, g_ref, q_ref, k_ref, v_ref, *rest, tq, dh, lam_init, qi):
    o_ref, s_ref, acc_ref = rest[-3:]
    scale = dh ** -0.5 * math.log2(math.e)
    q = q_ref[...]
    run_max = [None, None]
    for kj in range(qi + 1):
        k = k_ref[pl.ds(kj * tq, tq), :]
        for c in range(2):
            s = lax.dot_general(q[:, c * dh:(c + 1) * dh], k[:, c * dh:(c + 1) * dh], _NT,
                                preferred_element_type=_F32) * scale
            if kj == qi:
                row_chunk = lax.broadcasted_iota(jnp.int32, (tq, tq), 0) // CHUNK
                col_chunk = lax.broadcasted_iota(jnp.int32, (tq, tq), 1) // CHUNK
                s = jnp.where(col_chunk <= row_chunk, s, -jnp.inf)
            s_ref[pl.ds(c * tq, tq), pl.ds(kj * tq, tq)] = s
            part = s[:, :LANES]
            for j in range(1, tq // LANES):
                part = jnp.maximum(part, s[:, j * LANES:(j + 1) * LANES])
            run_max[c] = part if run_max[c] is None else jnp.maximum(run_max[c], part)
    m = jnp.concatenate(run_max, axis=0)
    row_max = jnp.broadcast_to(jnp.max(m, axis=-1, keepdims=True), m.shape)
    total = None
    for kj in range(qi + 1):
        ps = []
        for j in range(tq // LANES):
            p = jnp.exp2(s_ref[:, pl.ds(kj * tq + j * LANES, LANES)] - row_max)
            total = p if total is None else total + p
            ps.append(p.astype(_BF16))
        contrib = jnp.dot(jnp.concatenate(ps, axis=1), v_ref[pl.ds(kj * tq, tq), :], preferred_element_type=_F32)
        if kj == 0:
            acc_ref[...] = contrib
        else:
            acc_ref[...] += contrib
    lam = _lambda(lq1_ref, lk1_ref, lq2_ref, lk2_ref, lam_init)
    o = acc_ref[...] / jnp.sum(total, axis=-1, keepdims=True)
    o = o[:tq] - lam * o[tq:]
    o_ref[...] = _sub_layernorm(o, g_ref, lam_init).astype(o_ref.dtype)


def _attn_prompt(q, k, v, lam_params, subln_g, *, batch, seq, n_heads, dh, lam_init, tq=512):
    hw = 2 * dh
    assert seq % tq == 0 and tq % CHUNK == 0
    nq = seq // tq
    small = pl.BlockSpec((1, dh), lambda b, h: (0, 0))
    out = None
    for qi in range(nq):
        tile = pl.BlockSpec((tq, hw), lambda b, h, qi=qi: (b * nq + qi, h))
        in_specs = [small, small, small, small,
                    pl.BlockSpec((1, hw), lambda b, h: (0, 0)),
                    tile,
                    pl.BlockSpec((seq, hw), lambda b, h: (b, h)),
                    pl.BlockSpec((seq, hw), lambda b, h: (b, h))]
        args = [*lam_params, subln_g.reshape(1, hw), q, k, v]
        if out is not None:
            in_specs.append(pl.BlockSpec(memory_space=pl.ANY))
            args.append(out)
        out = pl.pallas_call(
            functools.partial(_attn_prompt_kernel, tq=tq, dh=dh, lam_init=lam_init, qi=qi),
            grid=(batch, n_heads),
            in_specs=in_specs,
            out_specs=tile,
            out_shape=jax.ShapeDtypeStruct(q.shape, _BF16),
            scratch_shapes=[pltpu.VMEM((2 * tq, (qi + 1) * tq), _F32), pltpu.VMEM((2 * tq, hw), _F32)],
            input_output_aliases={len(args) - 1: 0} if qi else {},
            compiler_params=_params("parallel", "parallel"),
            name="attn_prompt",
        )(*args)
    return out


def _attn_sample_kernel(lq1_ref, lk1_ref, lq2_ref, lk2_ref, g_ref, q_ref, kn_ref, vn_ref, ka_ref, kb_ref,
                        vlo_ref, vhi_ref, o_ref, *, dh, lam_init):
    scale = dh ** -0.5
    past = ka_ref.shape[0]
    hw = 2 * dh
    k_rows = [r.reshape(past * SUBLANES, dh) for r in (ka_ref, kb_ref)]
    v_rows = [r.reshape(past * SUBLANES, dh) for r in (vlo_ref, vhi_ref)]
    lam = _lambda(lq1_ref, lk1_ref, lq2_ref, lk2_ref, lam_init)
    per_block = SUBLANES // 2
    for hl in range(SUBLANES):
        cols = slice(hl * hw, (hl + 1) * hw)
        q = q_ref[:, cols]
        kn = kn_ref[:, cols]
        a_p = a_n = None
        for c in range(2):
            row = (hl % per_block) * 2 + c
            kp = k_rows[hl // per_block][pl.ds(row, past, stride=SUBLANES), :].astype(_BF16)
            qc = q[:, c * dh:(c + 1) * dh]
            s_p = lax.dot_general(qc, kp, _NT, preferred_element_type=_F32) * scale
            s_n = lax.dot_general(qc, kn[:, c * dh:(c + 1) * dh], _NT, preferred_element_type=_F32) * scale
            m = jnp.maximum(jnp.max(s_p, axis=-1, keepdims=True), jnp.max(s_n, axis=-1, keepdims=True))
            e_p = jnp.exp(s_p - m)
            e_n = jnp.exp(s_n - m)
            denom = jnp.sum(e_p, axis=-1, keepdims=True) + jnp.sum(e_n, axis=-1, keepdims=True)
            p_p = e_p / denom
            p_n = e_n / denom
            if c == 0:
                a_p, a_n = p_p, p_n
            else:
                a_p, a_n = a_p - lam * p_p, a_n - lam * p_n
        vp = jnp.concatenate([v[pl.ds(hl, past, stride=SUBLANES), :] for v in v_rows], axis=1).astype(_BF16)
        o = (jnp.dot(a_p.astype(_BF16), vp, preferred_element_type=_F32)
             + jnp.dot(a_n.astype(_BF16), vn_ref[:, cols], preferred_element_type=_F32))
        o_ref[:, cols] = _sub_layernorm(o, g_ref, lam_init).astype(o_ref.dtype)


def _attn_sample(q, k, v, cache_k, cache_v, layer, lam_params, subln_g, *, batch, seq, n_heads, dh, lam_init):
    hw = 2 * dh
    past = cache_k.shape[2]
    assert n_heads % SUBLANES == 0 and dh == LANES
    gw = SUBLANES * hw
    ck = cache_k.reshape(-1, past, n_heads * 2, dh)
    cv = cache_v.reshape(-1, past, n_heads, hw)
    base = layer * batch
    small = pl.BlockSpec((1, dh), lambda b, g: (0, 0))
    new = pl.BlockSpec((seq, gw), lambda b, g: (b, g))
    old = lambda sub, lane: pl.BlockSpec((None, past, SUBLANES, dh), lambda b, g: (base + b, 0, sub(g), lane))
    return pl.pallas_call(
        functools.partial(_attn_sample_kernel, dh=dh, lam_init=lam_init),
        grid=(batch, n_heads // SUBLANES),
        in_specs=[small, small, small, small, pl.BlockSpec((1, hw), lambda b, g: (0, 0)),
                  new, new, new,
                  old(lambda g: 2 * g, 0), old(lambda g: 2 * g + 1, 0), old(lambda g: g, 0), old(lambda g: g, 1)],
        out_specs=new,
        out_shape=jax.ShapeDtypeStruct(q.shape, _BF16),
        compiler_params=_params("parallel", "parallel"),
        name="attn_sample",
    )(*lam_params, subln_g.reshape(1, hw), q, k, v, ck, ck, cv, cv)


def _conv_ln_silu(ext_ref, c_ref, w_ref, bdw_ref, g_ref, b_ref, o_ref, *, rows):
    nslab = c_ref.shape[0]
    d = nslab * LANES
    first = CONV_HALO - (CONV_WIDTH - 1)

    def slab(si, carry):
        lanes = pl.ds(pl.multiple_of(si * LANES, LANES), LANES)
        taps = [jnp.broadcast_to(w_ref[pl.ds(j, 1), lanes], (SUBLANES, LANES)) for j in range(CONV_WIDTH)]
        bias = jnp.broadcast_to(bdw_ref[:, lanes], (SUBLANES, LANES))
        for r0 in range(0, rows, 2 * SUBLANES):
            for parity in range(2):
                acc = bias
                for j in range(CONV_WIDTH):
                    acc = acc + taps[j] * ext_ref[si, pl.ds(first + r0 + parity + j, SUBLANES, stride=2), :]
                c_ref[si, pl.ds(r0 + parity, SUBLANES, stride=2), :] = acc
        return carry

    lax.fori_loop(0, nslab, slab, 0)

    total = c_ref[0]
    for si in range(1, nslab):
        total = total + c_ref[si]
    mu = jnp.broadcast_to(jnp.sum(total, axis=-1, keepdims=True) * (1.0 / d), (rows, LANES))
    total = None
    for si in range(nslab):
        cc = c_ref[si] - mu
        total = cc * cc if total is None else total + cc * cc
    var = jnp.sum(total, axis=-1, keepdims=True) * (1.0 / d)
    inv = jnp.broadcast_to(lax.rsqrt(var + EPS), (rows, LANES))
    for si in range(nslab):
        lanes = slice(si * LANES, (si + 1) * LANES)
        y = (c_ref[si] - mu) * inv * g_ref[:, lanes] + b_ref[:, lanes]
        o_ref[:, lanes] = (y * jax.nn.sigmoid(y)).astype(o_ref.dtype)


def _conv_prompt_kernel(x_ref, halo_ref, w_ref, bdw_ref, g_ref, b_ref, o_ref, ext_ref, c_ref, *, rows):
    i = pl.program_id(1)
    for si in range(ext_ref.shape[0]):
        lanes = slice(si * LANES, (si + 1) * LANES)
        halo = halo_ref[:, lanes]
        ext_ref[si, pl.ds(0, CONV_HALO), :] = jnp.where(i == 0, jnp.zeros_like(halo), halo)
        ext_ref[si, pl.ds(CONV_HALO, rows), :] = x_ref[:, lanes]
    _conv_ln_silu(ext_ref, c_ref, w_ref, bdw_ref, g_ref, b_ref, o_ref, rows=rows)


def _conv_sample_kernel(x_ref, st_ref, w_ref, bdw_ref, g_ref, b_ref, o_ref, ext_ref, c_ref, *, rows):
    ctx = CONV_WIDTH - 1
    for si in range(ext_ref.shape[0]):
        lanes = slice(si * LANES, (si + 1) * LANES)
        ext_ref[si, pl.ds(CONV_HALO - ctx, ctx), :] = st_ref[0, :, lanes]
        ext_ref[si, pl.ds(CONV_HALO, rows), :] = x_ref[:, lanes]
    _conv_ln_silu(ext_ref, c_ref, w_ref, bdw_ref, g_ref, b_ref, o_ref, rows=rows)


def _conv_module(x, state, w_dw, b_dw, ln_g, ln_b, *, batch, seq):
    d = x.shape[1]
    vec = lambda a: a.reshape(1, d)
    if state is None:
        rows = 128
        nt = seq // rows
        grid = (batch, nt)
        per = rows // CONV_HALO
        const = lambda b, i: (0, 0)
        in_specs = [pl.BlockSpec((rows, d), lambda b, i: (b * nt + i, 0)),
                    pl.BlockSpec((CONV_HALO, d), lambda b, i: (jnp.maximum((b * nt + i) * per - 1, 0), 0))]
        out_spec = pl.BlockSpec((rows, d), lambda b, i: (b * nt + i, 0))
        body = functools.partial(_conv_prompt_kernel, rows=rows)
        args = [x, x]
        sem = ("parallel", "arbitrary")
        name = "conv_prompt"
    else:
        rows = seq
        grid = (batch,)
        const = lambda b: (0, 0)
        in_specs = [pl.BlockSpec((rows, d), lambda b: (b, 0)),
                    pl.BlockSpec((1, CONV_WIDTH - 1, d), lambda b: (b, 0, 0))]
        out_spec = pl.BlockSpec((rows, d), lambda b: (b, 0))
        body = functools.partial(_conv_sample_kernel, rows=rows)
        args = [x, state]
        sem = ("parallel",)
        name = "conv_sample"
    in_specs += [pl.BlockSpec((CONV_WIDTH, d), const)] + [pl.BlockSpec((1, d), const)] * 3
    return pl.pallas_call(
        body,
        grid=grid,
        in_specs=in_specs,
        out_specs=out_spec,
        out_shape=jax.ShapeDtypeStruct(x.shape, _BF16),
        scratch_shapes=[pltpu.VMEM((d // LANES, CONV_HALO + rows, LANES), _F32),
                        pltpu.VMEM((d // LANES, rows, LANES), _F32)],
        compiler_params=_params(*sem),
        name=name,
    )(*args, w_dw, vec(b_dw), vec(ln_g), vec(ln_b))


def _top_rows(s, k):
    n = s.shape[0]
    iota = lax.broadcasted_iota(jnp.int32, s.shape, 0).astype(_F32)
    vals, idxs = [], []
    for _ in range(k):
        m = jnp.max(s, axis=0, keepdims=True)
        am = jnp.min(jnp.where(s == m, iota, float(n)), axis=0, keepdims=True)
        vals.append(m)
        idxs.append(am)
        s = jnp.where(iota == am, -jnp.inf, s)
    return jnp.concatenate(vals, axis=0), jnp.concatenate(idxs, axis=0)


def _pick_row(rank, table):
    out = jnp.zeros_like(rank)
    for k in range(table.shape[0]):
        out = jnp.where(rank == float(k), table[k:k + 1], out)
    return out


def _router_kernel(q_ref, keys_ref, a_ref, b_ref, g_ref, *, half):
    kk = PEER_TOPK
    lim = 4
    assert lim * lim >= kk
    q = q_ref[...]
    tops = []
    for c in range(2):
        s = lax.dot_general(keys_ref[0, c], q[:, c * half:(c + 1) * half], _NT, preferred_element_type=_F32)
        tops.append(_top_rows(s, kk))
    (s0, i0), (s1, i1) = tops
    rank = lax.broadcasted_iota(jnp.int32, s0.shape, 0)
    rank_f = rank.astype(_F32)
    never = float(kk * kk)
    cands, flats = [], []
    for k1 in range(lim):
        ok = rank < kk // (k1 + 1)
        cands.append(jnp.where(ok, s0[k1:k1 + 1] + s1, -jnp.inf))
        flats.append(jnp.where(ok, k1 * kk + rank_f, never))
    for k2 in range(lim):
        if kk // (k2 + 1) <= lim:
            continue
        ok = (rank >= lim) & (rank < kk // (k2 + 1))
        cands.append(jnp.where(ok, s0 + s1[k2:k2 + 1], -jnp.inf))
        flats.append(jnp.where(ok, rank_f * kk + k2, never))
    cand = jnp.concatenate(cands, axis=0)
    flat = jnp.concatenate(flats, axis=0)
    best, pos = [], []
    for _ in range(kk):
        m = jnp.max(cand, axis=0, keepdims=True)
        p = jnp.min(jnp.where(cand == m, flat, never), axis=0, keepdims=True)
        best.append(m)
        pos.append(p)
        cand = jnp.where(flat == p, -jnp.inf, cand)
    best = jnp.concatenate(best, axis=0)
    pos = jnp.concatenate(pos, axis=0)
    k1 = jnp.floor(pos * (1.0 / kk))
    k2 = pos - k1 * kk
    e = jnp.exp(best - best[0:1])
    g_ref[...] = e / jnp.sum(e, axis=0, keepdims=True)
    a_ref[...] = _pick_row(k1, i0).astype(jnp.int32)
    b_ref[...] = _pick_row(k2, i1).astype(jnp.int32)


def _peer_route(q, sub_keys, *, tm=1024):
    n = q.shape[0]
    tm = min(tm, n)
    heads, _, n_keys, half = sub_keys.shape
    nsel = heads * PEER_TOPK
    out_spec = pl.BlockSpec((PEER_TOPK, tm), lambda i, h: (h, i))
    return pl.pallas_call(
        functools.partial(_router_kernel, half=half),
        grid=(n // tm, heads),
        in_specs=[pl.BlockSpec((tm, 2 * half), lambda i, h: (i, h)),
                  pl.BlockSpec((1, 2, n_keys, half), lambda i, h: (h, 0, 0, 0))],
        out_specs=[out_spec, out_spec, out_spec],
        out_shape=[jax.ShapeDtypeStruct((nsel, n), jnp.int32), jax.ShapeDtypeStruct((nsel, n), jnp.int32),
                   jax.ShapeDtypeStruct((nsel, n), _F32)],
        compiler_params=_params("parallel", "parallel"),
        name="peer_route",
    )(q, sub_keys)


def _gate_matrix_kernel(a_ref, b_ref, g_ref, o_ref, at_ref, bt_ref, gt_ref, s_ref, *, n_keys, tg):
    at_ref[...] = a_ref[...].T
    bt_ref[...] = b_ref[...].T
    gt_ref[...] = g_ref[...].T
    nsel = at_ref.shape[1]
    pitch = _gate_pitch(n_keys)
    key_id = lax.broadcasted_iota(jnp.int32, (n_keys, nsel), 0)

    def token(t, carry):
        row = pl.ds(t, 1)
        lhs = jnp.where(at_ref[row, :] == key_id, gt_ref[row, :], 0.0).astype(_BF16)
        rhs = jnp.where(bt_ref[row, :] == key_id, 1.0, 0.0).astype(_BF16)
        s_ref[pl.ds(pl.multiple_of(t * pitch, SUBLANES), n_keys), :] = lax.dot_general(
            lhs, rhs, _NT, preferred_element_type=_F32)
        return carry

    lax.fori_loop(0, tg, token, 0, unroll=32)
    for i in range(n_keys):
        o_ref[:, i * n_keys:(i + 1) * n_keys] = s_ref[pl.ds(i, tg, stride=pitch), :].astype(o_ref.dtype)


def _gate_pitch(n_keys):
    return n_keys + SUBLANES if (n_keys // SUBLANES) % 2 == 0 else n_keys


def _gate_matrix(sel_a, sel_b, gate, *, n_keys, tg=128):
    nsel, n = sel_a.shape
    in_spec = pl.BlockSpec((nsel, tg), lambda i: (0, i))
    return pl.pallas_call(
        functools.partial(_gate_matrix_kernel, n_keys=n_keys, tg=tg),
        grid=(n // tg,),
        in_specs=[in_spec, in_spec, in_spec],
        out_specs=pl.BlockSpec((tg, n_keys * n_keys), lambda i: (i, 0)),
        out_shape=jax.ShapeDtypeStruct((n, n_keys * n_keys), _BF16),
        scratch_shapes=[pltpu.VMEM((tg, nsel), jnp.int32), pltpu.VMEM((tg, nsel), jnp.int32),
                        pltpu.VMEM((tg, nsel), _F32), pltpu.VMEM((tg * _gate_pitch(n_keys), n_keys), _F32)],
        compiler_params=_params("parallel"),
        name="peer_gates",
    )(sel_a, sel_b, gate)


def _peer_weights_kernel(x_ref, u_ref, gates_ref, *rest):
    n_side = (len(rest) - 1) // 2
    o_ref = rest[n_side]
    act = lax.dot_general(x_ref[...], u_ref[...].astype(x_ref.dtype), _NT, preferred_element_type=_F32)
    gelu = 0.5 * act * (1.0 + lax.erf(act * (2.0 ** -0.5)))
    o_ref[...] = (gelu * gates_ref[...].astype(_F32)).astype(o_ref.dtype)
    for src, dst in zip(rest[:n_side], rest[n_side + 1:]):
        dst[...] = src[...].astype(dst.dtype)


def _peer_weights(x, u, layer, gates, side=(), *, tm=1024, te=512):
    n, d = x.shape
    n_exp = u.shape[1]
    tm = min(tm, n)
    grid = (n // tm, n_exp // te)
    steps = grid[0] * grid[1]
    side_in, side_out, side_shape = [], [], []
    for w, wl in side:
        _, rows, cols = w.shape
        chunk = rows // steps
        assert chunk * steps == rows and chunk % (2 * SUBLANES) == 0
        step = lambda i, j: i * grid[1] + j
        side_in.append(pl.BlockSpec((None, chunk, cols), lambda i, j, wl=wl: (wl, step(i, j), 0)))
        side_out.append(pl.BlockSpec((chunk, cols), lambda i, j: (step(i, j), 0)))
        side_shape.append(jax.ShapeDtypeStruct((rows, cols), _BF16))
    outs = pl.pallas_call(
        _peer_weights_kernel,
        grid=grid,
        in_specs=[pl.BlockSpec((tm, d), lambda i, j: (i, 0)),
                  pl.BlockSpec((None, te, d), lambda i, j: (layer, j, 0)),
                  pl.BlockSpec((tm, te), lambda i, j: (i, j))] + side_in,
        out_specs=[pl.BlockSpec((tm, te), lambda i, j: (i, j))] + side_out,
        out_shape=[jax.ShapeDtypeStruct((n, n_exp), _BF16)] + side_shape,
        compiler_params=_params("arbitrary", "arbitrary"),
        name="peer_weights",
    )(x, u, gates, *[w for w, _ in side])
    return outs[0], list(outs[1:])


def _peer_route_gates(h, norm_g, w_q, sub_keys):
    assert h.shape[0] % LANES == 0
    hn = _rmsnorm(h, norm_g, _BF16)
    q = _matmul(hn, w_q, out_dtypes=(_BF16,), name="peer_query")
    sel_a, sel_b, gate = _peer_route(q, sub_keys)
    return hn, _gate_matrix(sel_a, sel_b, gate, n_keys=sub_keys.shape[2])


def _peer_mix(weights, v_tab, h):
    return _matmul(weights, v_tab, resid=h, tiles=(1024, 1024, 2048), name="peer_mix")


def kernel(x_prompt, x_sample, cache_k, cache_v, state_conv, mixer_norm_g, ffn_norm_g, final_norm_g, w_qkv, lambda_q1, lambda_k1, lambda_q2, lambda_k2, subln_g, w_o, w_pw1, b_pw1, w_dw, b_dw, conv_ln_g, conv_ln_b, w_pw2, b_pw2, peer_wq, peer_sub_keys, peer_u, peer_v):
    batch, seq, d = x_prompt.shape
    dbatch, dseq, _ = x_sample.shape
    depth = mixer_norm_g.shape[0]
    n_heads, dh = cache_k.shape[3], cache_k.shape[5]
    past = cache_k.shape[2]
    assert dseq == CHUNK and past % CHUNK == 0
    sets = [dict(h=x_prompt.reshape(batch * seq, d), batch=batch, seq=seq, prompt=True),
            dict(h=x_sample.reshape(dbatch * dseq, d), batch=dbatch, seq=dseq, prompt=False)]
    new_k, new_v, new_conv = [[], []], [[], []], [[], []]
    early = {}

    for i in range(depth):
        if i % 2 == 0:
            a = i // 2
            lam_init = 0.8 - 0.6 * math.exp(-0.3 * i)
            wq, wk, wv = (_layer_bf16(w_qkv, a, s * d, d) for s in range(3))
            wo = _layer_bf16(w_o, a)
            lam_params = [p[a].reshape(1, dh) for p in (lambda_q1, lambda_k1, lambda_q2, lambda_k2)]
            for si, st in enumerate(sets):
                hn = _rmsnorm(st["h"], mixer_norm_g[i], _BF16)
                q = _matmul(hn, wq, out_dtypes=(_BF16,), name="q_proj")
                k32, k16 = _matmul(hn, wk, out_dtypes=(_F32, _BF16), lane_rows_first=True,
                                   tiles=(512, SUBLANES * LANES, d), name="k_proj")
                v32, v16 = _matmul(hn, wv, out_dtypes=(_F32, _BF16), name="v_proj")
                kw = dict(batch=st["batch"], seq=st["seq"], n_heads=n_heads, dh=dh, lam_init=lam_init)
                if st["prompt"]:
                    o = _attn_prompt(q, k16, v16, lam_params, subln_g[a], **kw)
                else:
                    o = _attn_sample(q, k16, v16, cache_k, cache_v, a, lam_params, subln_g[a], **kw)
                st["h"] = _matmul(o, wo, resid=st["h"], name="o_proj")
                new_k[si].append(k32.reshape(st["batch"], st["seq"], n_heads, 2, dh))
                new_v[si].append(v32.reshape(st["batch"], st["seq"], n_heads, 2 * dh))
        else:
            c = i // 2
            w1, w2 = early.pop(("w_pw1", c), None), early.pop(("w_pw2", c), None)
            if w1 is None:
                w1, w2 = _layer_bf16(w_pw1, c), _layer_bf16(w_pw2, c)
            for si, st in enumerate(sets):
                hn = _rmsnorm(st["h"], mixer_norm_g[i], _BF16)
                glu = _glu_matmul(hn, w1, b_pw1[c])
                state = None if st["prompt"] else state_conv[c]
                z = _conv_module(glu, state, w_dw[c], b_dw[c], conv_ln_g[c], conv_ln_b[c],
                                 batch=st["batch"], seq=st["seq"])
                st["h"] = _matmul(z, w2, bias=b_pw2[c], resid=st["h"], name="pw2")
                glu3 = glu.reshape(st["batch"], st["seq"], d)
                if st["prompt"]:
                    new_conv[si].append(glu3[:, seq - (CONV_WIDTH - 1):])
                else:
                    ctx = jnp.concatenate([state, glu3], axis=1)
                    new_conv[si].append(ctx[:, -(CONV_WIDTH - 1):])
        wpq = early.pop(("peer_wq", i), None)
        if wpq is None:
            wpq = _layer_bf16(peer_wq, i)
        keys = peer_sub_keys[i].astype(_BF16)
        jobs = [("peer_v", peer_v, i)]
        if i + 1 < depth:
            jobs.append(("peer_wq", peer_wq, i + 1))
            if (i + 1) % 2 == 1:
                jobs += [("w_pw1", w_pw1, (i + 1) // 2), ("w_pw2", w_pw2, (i + 1) // 2)]
        v_tab = None
        for st in sets:
            hn, gates = _peer_route_gates(st["h"], ffn_norm_g[i], wpq, keys)
            side = [(w, wl) for _, w, wl in jobs] if st["prompt"] else []
            weights, copies = _peer_weights(hn, peer_u, i, gates, side)
            if st["prompt"]:
                v_tab = copies[0]
                early.update({(name, wl): cp for (name, _, wl), cp in zip(jobs[1:], copies[1:])})
            st["h"] = _peer_mix(weights, v_tab, st["h"])

    y_prompt = _rmsnorm(sets[0]["h"], final_norm_g, _F32).reshape(batch, seq, d)
    y_sample = _rmsnorm(sets[1]["h"], final_norm_g, _F32).reshape(dbatch, dseq, d)
    return (y_prompt, y_sample, jnp.stack(new_k[0]), jnp.stack(new_v[0]), jnp.stack(new_conv[0]),
            jnp.stack(new_k[1]), jnp.stack(new_v[1]), jnp.stack(new_conv[1]))
```

```python
import functools
import math

import jax
import jax.numpy as jnp
from jax import lax
from jax.experimental import pallas as pl
from jax.experimental.pallas import tpu as pltpu

EPS = 1e-6
CHUNK = 64
CONV_WIDTH = 31
CONV_HALO = 32
PEER_TOPK = 16
LANES = 128
SUBLANES = 8
VMEM_LIMIT = 56 * 1024 * 1024

_NT = (((1,), (1,)), ((), ()))
_F32 = jnp.float32
_BF16 = jnp.bfloat16


def _params(*sem):
    return pltpu.CompilerParams(dimension_semantics=sem, vmem_limit_bytes=VMEM_LIMIT)


def _cast_kernel(x_ref, o_ref):
    o_ref[...] = x_ref[0].astype(o_ref.dtype)


def _layer_bf16(w, layer, col0=0, ncols=None):
    _, rows, cols = w.shape
    ncols = cols if ncols is None else ncols
    tr, tc = min(512, rows), min(2048, ncols)
    assert col0 % tc == 0
    c0 = col0 // tc
    return pl.pallas_call(
        _cast_kernel,
        grid=(rows // tr, ncols // tc),
        in_specs=[pl.BlockSpec((1, tr, tc), lambda i, j: (layer, i, j + c0))],
        out_specs=pl.BlockSpec((tr, tc), lambda i, j: (i, j)),
        out_shape=jax.ShapeDtypeStruct((rows, ncols), _BF16),
        compiler_params=_params("parallel", "parallel"),
        name="to_bf16",
    )(w)


def _rmsnorm_kernel(x_ref, g_ref, o_ref):
    x = x_ref[...]
    ms = jnp.mean(x * x, axis=-1, keepdims=True)
    o_ref[...] = (x * lax.rsqrt(ms + EPS) * g_ref[...]).astype(o_ref.dtype)


def _rmsnorm(x, g, out_dtype, tr=256):
    m, d = x.shape
    return pl.pallas_call(
        _rmsnorm_kernel,
        grid=(m // tr,),
        in_specs=[pl.BlockSpec((tr, d), lambda i: (i, 0)), pl.BlockSpec((1, d), lambda i: (0, 0))],
        out_specs=pl.BlockSpec((tr, d), lambda i: (i, 0)),
        out_shape=jax.ShapeDtypeStruct((m, d), out_dtype),
        compiler_params=_params("parallel"),
        name="rmsnorm",
    )(x, g.reshape(1, d))


def _mm_kernel(*refs, has_bias, has_resid, n_out, multi_k):
    x_ref, w_ref = refs[0], refs[1]
    pos = 2
    b_ref = r_ref = None
    if has_bias:
        b_ref = refs[pos]
        pos += 1
    if has_resid:
        r_ref = refs[pos]
        pos += 1
    out_refs = refs[pos:pos + n_out]

    def finish(r):
        if has_bias:
            r = r + b_ref[...]
        if has_resid:
            r = r + r_ref[...]
        for o in out_refs:
            if len(o.shape) == 2:
                o[...] = r.astype(o.dtype)
            else:
                rows, groups, _ = o.shape
                flat = o.reshape(rows * groups, LANES)
                for c in range(groups):
                    flat[pl.ds(c, rows, stride=groups), :] = r[:, c * LANES:(c + 1) * LANES].astype(o.dtype)

    def product():
        return jnp.dot(x_ref[...], w_ref[...], preferred_element_type=_F32)

    if not multi_k:
        finish(product())
        return
    acc_ref = refs[pos + n_out]
    k = pl.program_id(2)

    @pl.when(k == 0)
    def _():
        acc_ref[...] = product()

    @pl.when(k > 0)
    def _():
        acc_ref[...] += product()

    @pl.when(k == pl.num_programs(2) - 1)
    def _():
        finish(acc_ref[...])


def _mm_tiles(m, n, k):
    return min(1024, m), min(512, n), min(4096, k)


def _matmul(x, w, *, bias=None, resid=None, out_dtypes=(_F32,), lane_rows_first=False, tiles=None, name="matmul"):
    m, kdim = x.shape
    n = w.shape[-1]
    tm, tn, tk = tiles or _mm_tiles(m, n, kdim)
    tm, tn, tk = min(tm, m), min(tn, n), min(tk, kdim)
    multi_k = kdim > tk
    out_specs = [pl.BlockSpec((tm, tn), lambda i, j, k: (i, j)) for _ in out_dtypes]
    out_shape = [jax.ShapeDtypeStruct((m, n), dt) for dt in out_dtypes]
    if lane_rows_first:
        assert (tn // LANES) % SUBLANES == 0 or tn == n
        out_specs[0] = pl.BlockSpec((tm, tn // LANES, LANES), lambda i, j, k: (i, j, 0))
        out_shape[0] = jax.ShapeDtypeStruct((m, n // LANES, LANES), out_dtypes[0])
    in_specs = [pl.BlockSpec((tm, tk), lambda i, j, k: (i, k)),
                pl.BlockSpec((tk, tn), lambda i, j, k: (k, j))]
    args = [x, w]
    if bias is not None:
        in_specs.append(pl.BlockSpec((1, tn), lambda i, j, k: (0, j)))
        args.append(bias.reshape(1, n))
    if resid is not None:
        in_specs.append(pl.BlockSpec((tm, tn), lambda i, j, k: (i, j)))
        args.append(resid)
    outs = pl.pallas_call(
        functools.partial(_mm_kernel, has_bias=bias is not None, has_resid=resid is not None,
                          n_out=len(out_dtypes), multi_k=multi_k),
        grid=(m // tm, n // tn, kdim // tk),
        in_specs=in_specs,
        out_specs=out_specs,
        out_shape=out_shape,
        scratch_shapes=[pltpu.VMEM((tm, tn), _F32)] if multi_k else [],
        compiler_params=_params("parallel", "parallel", "arbitrary"),
        name=name,
    )(*args)
    return outs if len(out_dtypes) > 1 else outs[0]


def _glu_kernel(x_ref, wa_ref, wg_ref, ba_ref, bg_ref, o_ref):
    x = x_ref[...]
    a = jnp.dot(x, wa_ref[...], preferred_element_type=_F32) + ba_ref[...]
    gate = jnp.dot(x, wg_ref[...], preferred_element_type=_F32) + bg_ref[...]
    o_ref[...] = a * jax.nn.sigmoid(gate)


def _glu_matmul(x, w, b):
    m, kdim = x.shape
    n = w.shape[1] // 2
    tm, tn, tk = _mm_tiles(m, n, kdim)
    assert tk == kdim
    nj = n // tn
    b2 = b.reshape(1, 2 * n)
    return pl.pallas_call(
        _glu_kernel,
        grid=(m // tm, nj),
        in_specs=[pl.BlockSpec((tm, kdim), lambda i, j: (i, 0)),
                  pl.BlockSpec((kdim, tn), lambda i, j: (0, j)),
                  pl.BlockSpec((kdim, tn), lambda i, j: (0, j + nj)),
                  pl.BlockSpec((1, tn), lambda i, j: (0, j)),
                  pl.BlockSpec((1, tn), lambda i, j: (0, j + nj))],
        out_specs=pl.BlockSpec((tm, tn), lambda i, j: (i, j)),
        out_shape=jax.ShapeDtypeStruct((m, n), _F32),
        compiler_params=_params("parallel", "parallel"),
        name="glu_matmul",
    )(x, w, w, b2, b2)


def _lambda(lq1_ref, lk1_ref, lq2_ref, lk2_ref, lam_init):
    s1 = jnp.sum(lq1_ref[...] * lk1_ref[...], axis=-1, keepdims=True)
    s2 = jnp.sum(lq2_ref[...] * lk2_ref[...], axis=-1, keepdims=True)
    return jnp.exp(s1) - jnp.exp(s2) + lam_init


def _sub_layernorm(o, g_ref, lam_init):
    ms = jnp.mean(o * o, axis=-1, keepdims=True)
    return (o * lax.rsqrt(ms + EPS)) * g_ref[...] * (1.0 - lam_init)


def _attn_prompt_kernel(lq1_ref, lk1_ref, lq2_ref, lk2_ref, g_ref, q_ref, k_ref, v_ref, *rest, tq, dh, lam_init, qi):
    o_ref, s_ref, acc_ref = rest[-3:]
    scale = dh ** -0.5 * math.log2(math.e)
    q = q_ref[...]
    run_max = [None, None]
    for kj in range(qi + 1):
        k = k_ref[pl.ds(kj * tq, tq), :]
        for c in range(2):
            s = lax.dot_general(q[:, c * dh:(c + 1) * dh], k[:, c * dh:(c + 1) * dh], _NT,
                                preferred_element_type=_F32) * scale
            if kj == qi:
                row_chunk = lax.broadcasted_iota(jnp.int32, (tq, tq), 0) // CHUNK
                col_chunk = lax.broadcasted_iota(jnp.int32, (tq, tq), 1) // CHUNK
                s = jnp.where(col_chunk <= row_chunk, s, -jnp.inf)
            s_ref[pl.ds(c * tq, tq), pl.ds(kj * tq, tq)] = s
            part = s[:, :LANES]
            for j in range(1, tq // LANES):
                part = jnp.maximum(part, s[:, j * LANES:(j + 1) * LANES])
            run_max[c] = part if run_max[c] is None else jnp.maximum(run_max[c], part)
    m = jnp.concatenate(run_max, axis=0)
    row_max = jnp.broadcast_to(jnp.max(m, axis=-1, keepdims=True), m.shape)
    total = None
    for kj in range(qi + 1):
        ps = []
        for j in range(tq // LANES):
            p = jnp.exp2(s_ref[:, pl.ds(kj * tq + j * LANES, LANES)] - row_max)
            total = p if total is None else total + p
            ps.append(p.astype(_BF16))
        contrib = jnp.dot(jnp.concatenate(ps, axis=1), v_ref[pl.ds(kj * tq, tq), :], preferred_element_type=_F32)
        if kj == 0:
            acc_ref[...] = contrib
        else:
            acc_ref[...] += contrib
    lam = _lambda(lq1_ref, lk1_ref, lq2_ref, lk2_ref, lam_init)
    o = acc_ref[...] / jnp.sum(total, axis=-1, keepdims=True)
    o = o[:tq] - lam * o[tq:]
    o_ref[...] = _sub_layernorm(o, g_ref, lam_init).astype(o_ref.dtype)


def _attn_prompt(q, k, v, lam_params, subln_g, *, batch, seq, n_heads, dh, lam_init, tq=512):
    hw = 2 * dh
    assert seq % tq == 0 and tq % CHUNK == 0
    nq = seq // tq
    small = pl.BlockSpec((1, dh), lambda b, h: (0, 0))
    out = None
    for qi in range(nq):
        tile = pl.BlockSpec((tq, hw), lambda b, h, qi=qi: (b * nq + qi, h))
        in_specs = [small, small, small, small,
                    pl.BlockSpec((1, hw), lambda b, h: (0, 0)),
                    tile,
                    pl.BlockSpec((seq, hw), lambda b, h: (b, h)),
                    pl.BlockSpec((seq, hw), lambda b, h: (b, h))]
        args = [*lam_params, subln_g.reshape(1, hw), q, k, v]
        if out is not None:
            in_specs.append(pl.BlockSpec(memory_space=pl.ANY))
            args.append(out)
        out = pl.pallas_call(
            functools.partial(_attn_prompt_kernel, tq=tq, dh=dh, lam_init=lam_init, qi=qi),
            grid=(batch, n_heads),
            in_specs=in_specs,
            out_specs=tile,
            out_shape=jax.ShapeDtypeStruct(q.shape, _BF16),
            scratch_shapes=[pltpu.VMEM((2 * tq, (qi + 1) * tq), _F32), pltpu.VMEM((2 * tq, hw), _F32)],
            input_output_aliases={len(args) - 1: 0} if qi else {},
            compiler_params=_params("parallel", "parallel"),
            name="attn_prompt",
        )(*args)
    return out


def _attn_sample_kernel(lq1_ref, lk1_ref, lq2_ref, lk2_ref, g_ref, q_ref, kn_ref, vn_ref, ka_ref, kb_ref,
                        vlo_ref, vhi_ref, o_ref, *, dh, lam_init):
    scale = dh ** -0.5
    past = ka_ref.shape[0]
    hw = 2 * dh
    k_rows = [r.reshape(past * SUBLANES, dh) for r in (ka_ref, kb_ref)]
    v_rows = [r.reshape(past * SUBLANES, dh) for r in (vlo_ref, vhi_ref)]
    lam = _lambda(lq1_ref, lk1_ref, lq2_ref, lk2_ref, lam_init)
    per_block = SUBLANES // 2
    for hl in range(SUBLANES):
        cols = slice(hl * hw, (hl + 1) * hw)
        q = q_ref[:, cols]
        kn = kn_ref[:, cols]
        a_p = a_n = None
        for c in range(2):
            row = (hl % per_block) * 2 + c
            kp = k_rows[hl // per_block][pl.ds(row, past, stride=SUBLANES), :].astype(_BF16)
            qc = q[:, c * dh:(c + 1) * dh]
            s_p = lax.dot_general(qc, kp, _NT, preferred_element_type=_F32) * scale
            s_n = lax.dot_general(qc, kn[:, c * dh:(c + 1) * dh], _NT, preferred_element_type=_F32) * scale
            m = jnp.maximum(jnp.max(s_p, axis=-1, keepdims=True), jnp.max(s_n, axis=-1, keepdims=True))
            e_p = jnp.exp(s_p - m)
            e_n = jnp.exp(s_n - m)
            denom = jnp.sum(e_p, axis=-1, keepdims=True) + jnp.sum(e_n, axis=-1, keepdims=True)
            p_p = e_p / denom
            p_n = e_n / denom
            if c == 0:
                a_p, a_n = p_p, p_n
            else:
                a_p, a_n = a_p - lam * p_p, a_n - lam * p_n
        vp = jnp.concatenate([v[pl.ds(hl, past, stride=SUBLANES), :] for v in v_rows], axis=1).astype(_BF16)
        o = (jnp.dot(a_p.astype(_BF16), vp, preferred_element_type=_F32)
             + jnp.dot(a_n.astype(_BF16), vn_ref[:, cols], preferred_element_type=_F32))
        o_ref[:, cols] = _sub_layernorm(o, g_ref, lam_init).astype(o_ref.dtype)


def _attn_sample(q, k, v, cache_k, cache_v, layer, lam_params, subln_g, *, batch, seq, n_heads, dh, lam_init):
    hw = 2 * dh
    past = cache_k.shape[2]
    assert n_heads % SUBLANES == 0 and dh == LANES
    gw = SUBLANES * hw
    ck = cache_k.reshape(-1, past, n_heads * 2, dh)
    cv = cache_v.reshape(-1, past, n_heads, hw)
    base = layer * batch
    small = pl.BlockSpec((1, dh), lambda b, g: (0, 0))
    new = pl.BlockSpec((seq, gw), lambda b, g: (b, g))
    old = lambda sub, lane: pl.BlockSpec((None, past, SUBLANES, dh), lambda b, g: (base + b, 0, sub(g), lane))
    return pl.pallas_call(
        functools.partial(_attn_sample_kernel, dh=dh, lam_init=lam_init),
        grid=(batch, n_heads // SUBLANES),
        in_specs=[small, small, small, small, pl.BlockSpec((1, hw), lambda b, g: (0, 0)),
                  new, new, new,
                  old(lambda g: 2 * g, 0), old(lambda g: 2 * g + 1, 0), old(lambda g: g, 0), old(lambda g: g, 1)],
        out_specs=new,
        out_shape=jax.ShapeDtypeStruct(q.shape, _BF16),
        compiler_params=_params("parallel", "parallel"),
        name="attn_sample",
    )(*lam_params, subln_g.reshape(1, hw), q, k, v, ck, ck, cv, cv)


def _conv_ln_silu(ext_ref, c_ref, w_ref, bdw_ref, g_ref, b_ref, o_ref, *, rows):
    nslab = c_ref.shape[0]
    d = nslab * LANES
    first = CONV_HALO - (CONV_WIDTH - 1)

    def slab(si, carry):
        lanes = pl.ds(pl.multiple_of(si * LANES, LANES), LANES)
        taps = [jnp.broadcast_to(w_ref[pl.ds(j, 1), lanes], (SUBLANES, LANES)) for j in range(CONV_WIDTH)]
        bias = jnp.broadcast_to(bdw_ref[:, lanes], (SUBLANES, LANES))
        for r0 in range(0, rows, 2 * SUBLANES):
            for parity in range(2):
                acc = bias
                for j in range(CONV_WIDTH):
                    acc = acc + taps[j] * ext_ref[si, pl.ds(first + r0 + parity + j, SUBLANES, stride=2), :]
                c_ref[si, pl.ds(r0 + parity, SUBLANES, stride=2), :] = acc
        return carry

    lax.fori_loop(0, nslab, slab, 0, unroll=2)

    total = c_ref[0]
    for si in range(1, nslab):
        total = total + c_ref[si]
    mu = jnp.broadcast_to(jnp.sum(total, axis=-1, keepdims=True) * (1.0 / d), (rows, LANES))
    total = None
    for si in range(nslab):
        cc = c_ref[si] - mu
        total = cc * cc if total is None else total + cc * cc
    var = jnp.sum(total, axis=-1, keepdims=True) * (1.0 / d)
    inv = jnp.broadcast_to(lax.rsqrt(var + EPS), (rows, LANES))
    for si in range(nslab):
        lanes = slice(si * LANES, (si + 1) * LANES)
        y = (c_ref[si] - mu) * inv * g_ref[:, lanes] + b_ref[:, lanes]
        o_ref[:, lanes] = (y * jax.nn.sigmoid(y)).astype(o_ref.dtype)


def _conv_prompt_kernel(x_ref, halo_ref, w_ref, bdw_ref, g_ref, b_ref, o_ref, ext_ref, c_ref, *, rows):
    i = pl.program_id(1)
    for si in range(ext_ref.shape[0]):
        lanes = slice(si * LANES, (si + 1) * LANES)
        halo = halo_ref[:, lanes]
        ext_ref[si, pl.ds(0, CONV_HALO), :] = jnp.where(i == 0, jnp.zeros_like(halo), halo)
        ext_ref[si, pl.ds(CONV_HALO, rows), :] = x_ref[:, lanes]
    _conv_ln_silu(ext_ref, c_ref, w_ref, bdw_ref, g_ref, b_ref, o_ref, rows=rows)


def _conv_sample_kernel(x_ref, st_ref, w_ref, bdw_ref, g_ref, b_ref, o_ref, ext_ref, c_ref, *, rows):
    ctx = CONV_WIDTH - 1
    for si in range(ext_ref.shape[0]):
        lanes = slice(si * LANES, (si + 1) * LANES)
        ext_ref[si, pl.ds(CONV_HALO - ctx, ctx), :] = st_ref[0, :, lanes]
        ext_ref[si, pl.ds(CONV_HALO, rows), :] = x_ref[:, lanes]
    _conv_ln_silu(ext_ref, c_ref, w_ref, bdw_ref, g_ref, b_ref, o_ref, rows=rows)


def _conv_module(x, state, w_dw, b_dw, ln_g, ln_b, *, batch, seq):
    d = x.shape[1]
    vec = lambda a: a.reshape(1, d)
    if state is None:
        rows = 128
        nt = seq // rows
        grid = (batch, nt)
        per = rows // CONV_HALO
        const = lambda b, i: (0, 0)
        in_specs = [pl.BlockSpec((rows, d), lambda b, i: (b * nt + i, 0)),
                    pl.BlockSpec((CONV_HALO, d), lambda b, i: (jnp.maximum((b * nt + i) * per - 1, 0), 0))]
        out_spec = pl.BlockSpec((rows, d), lambda b, i: (b * nt + i, 0))
        body = functools.partial(_conv_prompt_kernel, rows=rows)
        args = [x, x]
        sem = ("parallel", "arbitrary")
        name = "conv_prompt"
    else:
        rows = seq
        grid = (batch,)
        const = lambda b: (0, 0)
        in_specs = [pl.BlockSpec((rows, d), lambda b: (b, 0)),
                    pl.BlockSpec((1, CONV_WIDTH - 1, d), lambda b: (b, 0, 0))]
        out_spec = pl.BlockSpec((rows, d), lambda b: (b, 0))
        body = functools.partial(_conv_sample_kernel, rows=rows)
        args = [x, state]
        sem = ("parallel",)
        name = "conv_sample"
    in_specs += [pl.BlockSpec((CONV_WIDTH, d), const)] + [pl.BlockSpec((1, d), const)] * 3
    return pl.pallas_call(
        body,
        grid=grid,
        in_specs=in_specs,
        out_specs=out_spec,
        out_shape=jax.ShapeDtypeStruct(x.shape, _BF16),
        scratch_shapes=[pltpu.VMEM((d // LANES, CONV_HALO + rows, LANES), _F32),
                        pltpu.VMEM((d // LANES, rows, LANES), _F32)],
        compiler_params=_params(*sem),
        name=name,
    )(*args, w_dw, vec(b_dw), vec(ln_g), vec(ln_b))


def _top_rows(s, k):
    n = s.shape[0]
    iota = lax.broadcasted_iota(jnp.int32, s.shape, 0).astype(_F32)
    vals, idxs = [], []
    for _ in range(k):
        m = jnp.max(s, axis=0, keepdims=True)
        am = jnp.min(jnp.where(s == m, iota, float(n)), axis=0, keepdims=True)
        vals.append(m)
        idxs.append(am)
        s = jnp.where(iota == am, -jnp.inf, s)
    return jnp.concatenate(vals, axis=0), jnp.concatenate(idxs, axis=0)


def _pick_row(rank, table):
    out = jnp.zeros_like(rank)
    for k in range(table.shape[0]):
        out = jnp.where(rank == float(k), table[k:k + 1], out)
    return out


def _router_kernel(q_ref, keys_ref, a_ref, b_ref, g_ref, *, half):
    kk = PEER_TOPK
    lim = 4
    assert lim * lim >= kk
    q = q_ref[...]
    tops = []
    for c in range(2):
        s = lax.dot_general(keys_ref[0, c], q[:, c * half:(c + 1) * half], _NT, preferred_element_type=_F32)
        tops.append(_top_rows(s, kk))
    (s0, i0), (s1, i1) = tops
    rank = lax.broadcasted_iota(jnp.int32, s0.shape, 0)
    rank_f = rank.astype(_F32)
    never = float(kk * kk)
    cands, flats = [], []
    for k1 in range(lim):
        ok = rank < kk // (k1 + 1)
        cands.append(jnp.where(ok, s0[k1:k1 + 1] + s1, -jnp.inf))
        flats.append(jnp.where(ok, k1 * kk + rank_f, never))
    for k2 in range(lim):
        if kk // (k2 + 1) <= lim:
            continue
        ok = (rank >= lim) & (rank < kk // (k2 + 1))
        cands.append(jnp.where(ok, s0 + s1[k2:k2 + 1], -jnp.inf))
        flats.append(jnp.where(ok, rank_f * kk + k2, never))
    cand = jnp.concatenate(cands, axis=0)
    flat = jnp.concatenate(flats, axis=0)
    best, pos = [], []
    for _ in range(kk):
        m = jnp.max(cand, axis=0, keepdims=True)
        p = jnp.min(jnp.where(cand == m, flat, never), axis=0, keepdims=True)
        best.append(m)
        pos.append(p)
        cand = jnp.where(flat == p, -jnp.inf, cand)
    best = jnp.concatenate(best, axis=0)
    pos = jnp.concatenate(pos, axis=0)
    k1 = jnp.floor(pos * (1.0 / kk))
    k2 = pos - k1 * kk
    e = jnp.exp(best - best[0:1])
    g_ref[...] = e / jnp.sum(e, axis=0, keepdims=True)
    a_ref[...] = _pick_row(k1, i0).astype(jnp.int32)
    b_ref[...] = _pick_row(k2, i1).astype(jnp.int32)


def _peer_route(q, sub_keys, *, tm=1024):
    n = q.shape[0]
    tm = min(tm, n)
    heads, _, n_keys, half = sub_keys.shape
    nsel = heads * PEER_TOPK
    out_spec = pl.BlockSpec((PEER_TOPK, tm), lambda i, h: (h, i))
    return pl.pallas_call(
        functools.partial(_router_kernel, half=half),
        grid=(n // tm, heads),
        in_specs=[pl.BlockSpec((tm, 2 * half), lambda i, h: (i, h)),
                  pl.BlockSpec((1, 2, n_keys, half), lambda i, h: (h, 0, 0, 0))],
        out_specs=[out_spec, out_spec, out_spec],
        out_shape=[jax.ShapeDtypeStruct((nsel, n), jnp.int32), jax.ShapeDtypeStruct((nsel, n), jnp.int32),
                   jax.ShapeDtypeStruct((nsel, n), _F32)],
        compiler_params=_params("parallel", "parallel"),
        name="peer_route",
    )(q, sub_keys)


def _gate_matrix_kernel(a_ref, b_ref, g_ref, o_ref, at_ref, bt_ref, gt_ref, s_ref, *, n_keys, tg):
    at_ref[...] = a_ref[...].T
    bt_ref[...] = b_ref[...].T
    gt_ref[...] = g_ref[...].T
    nsel = at_ref.shape[1]
    pitch = _gate_pitch(n_keys)
    half = SUBLANES // 2
    key_id = lax.broadcasted_iota(jnp.int32, (n_keys, nsel), 0)
    key_id_odd = lax.broadcasted_iota(jnp.int32, (n_keys + SUBLANES, nsel), 0) - half

    def grid_of(t, ids):
        row = pl.ds(t, 1)
        lhs = jnp.where(at_ref[row, :] == ids, gt_ref[row, :], 0.0).astype(_BF16)
        rhs = jnp.where(bt_ref[row, :] == key_id, 1.0, 0.0).astype(_BF16)
        return lax.dot_general(lhs, rhs, _NT, preferred_element_type=_F32)

    def token_pair(p, carry):
        base = pl.multiple_of(p * (2 * pitch), SUBLANES)
        s_ref[pl.ds(base, n_keys), :] = grid_of(2 * p, key_id)
        odd = pl.multiple_of(base + pitch - half, SUBLANES)
        s_ref[pl.ds(odd, n_keys + SUBLANES), :] = grid_of(2 * p + 1, key_id_odd)
        return carry

    lax.fori_loop(0, tg // 2, token_pair, 0, unroll=16)
    for i in range(n_keys):
        o_ref[:, i * n_keys:(i + 1) * n_keys] = s_ref[pl.ds(i, tg, stride=pitch), :].astype(o_ref.dtype)


def _gate_pitch(n_keys):
    assert n_keys % SUBLANES == 0
    return n_keys + SUBLANES // 2


def _gate_matrix(sel_a, sel_b, gate, *, n_keys, tg=128):
    nsel, n = sel_a.shape
    in_spec = pl.BlockSpec((nsel, tg), lambda i: (0, i))
    return pl.pallas_call(
        functools.partial(_gate_matrix_kernel, n_keys=n_keys, tg=tg),
        grid=(n // tg,),
        in_specs=[in_spec, in_spec, in_spec],
        out_specs=pl.BlockSpec((tg, n_keys * n_keys), lambda i: (i, 0)),
        out_shape=jax.ShapeDtypeStruct((n, n_keys * n_keys), _BF16),
        scratch_shapes=[pltpu.VMEM((tg, nsel), jnp.int32), pltpu.VMEM((tg, nsel), jnp.int32),
                        pltpu.VMEM((tg, nsel), _F32), pltpu.VMEM((tg * _gate_pitch(n_keys), n_keys), _F32)],
        compiler_params=_params("parallel"),
        name="peer_gates",
    )(sel_a, sel_b, gate)


def _peer_weights_kernel(x_ref, u_ref, gates_ref, *rest):
    n_side = (len(rest) - 1) // 2
    o_ref = rest[n_side]
    act = lax.dot_general(x_ref[...], u_ref[...].astype(x_ref.dtype), _NT, preferred_element_type=_F32)
    gelu = 0.5 * act * (1.0 + lax.erf(act * (2.0 ** -0.5)))
    o_ref[...] = (gelu * gates_ref[...].astype(_F32)).astype(o_ref.dtype)
    for src, dst in zip(rest[:n_side], rest[n_side + 1:]):
        dst[...] = src[...].astype(dst.dtype)


def _peer_weights(x, u, layer, gates, side=(), *, tm=1024, te=512):
    n, d = x.shape
    n_exp = u.shape[1]
    tm = min(tm, n)
    grid = (n // tm, n_exp // te)
    steps = grid[0] * grid[1]
    side_in, side_out, side_shape = [], [], []
    for w, wl in side:
        _, rows, cols = w.shape
        chunk = rows // steps
        assert chunk * steps == rows and chunk % (2 * SUBLANES) == 0
        step = lambda i, j: i * grid[1] + j
        side_in.append(pl.BlockSpec((None, chunk, cols), lambda i, j, wl=wl: (wl, step(i, j), 0)))
        side_out.append(pl.BlockSpec((chunk, cols), lambda i, j: (step(i, j), 0)))
        side_shape.append(jax.ShapeDtypeStruct((rows, cols), _BF16))
    outs = pl.pallas_call(
        _peer_weights_kernel,
        grid=grid,
        in_specs=[pl.BlockSpec((tm, d), lambda i, j: (i, 0)),
                  pl.BlockSpec((None, te, d), lambda i, j: (layer, j, 0)),
                  pl.BlockSpec((tm, te), lambda i, j: (i, j))] + side_in,
        out_specs=[pl.BlockSpec((tm, te), lambda i, j: (i, j))] + side_out,
        out_shape=[jax.ShapeDtypeStruct((n, n_exp), _BF16)] + side_shape,
        compiler_params=_params("arbitrary", "arbitrary"),
        name="peer_weights",
    )(x, u, gates, *[w for w, _ in side])
    return outs[0], list(outs[1:])


def _peer_route_gates(h, norm_g, w_q, sub_keys):
    assert h.shape[0] % LANES == 0
    hn = _rmsnorm(h, norm_g, _BF16)
    q = _matmul(hn, w_q, out_dtypes=(_BF16,), name="peer_query")
    sel_a, sel_b, gate = _peer_route(q, sub_keys)
    return hn, _gate_matrix(sel_a, sel_b, gate, n_keys=sub_keys.shape[2])


def _peer_mix(weights, v_tab, h):
    return _matmul(weights, v_tab, resid=h, tiles=(1024, 1024, 2048), name="peer_mix")


def kernel(x_prompt, x_sample, cache_k, cache_v, state_conv, mixer_norm_g, ffn_norm_g, final_norm_g, w_qkv, lambda_q1, lambda_k1, lambda_q2, lambda_k2, subln_g, w_o, w_pw1, b_pw1, w_dw, b_dw, conv_ln_g, conv_ln_b, w_pw2, b_pw2, peer_wq, peer_sub_keys, peer_u, peer_v):
    batch, seq, d = x_prompt.shape
    dbatch, dseq, _ = x_sample.shape
    depth = mixer_norm_g.shape[0]
    n_heads, dh = cache_k.shape[3], cache_k.shape[5]
    past = cache_k.shape[2]
    assert dseq == CHUNK and past % CHUNK == 0
    sets = [dict(h=x_prompt.reshape(batch * seq, d), batch=batch, seq=seq, prompt=True),
            dict(h=x_sample.reshape(dbatch * dseq, d), batch=dbatch, seq=dseq, prompt=False)]
    new_k, new_v, new_conv = [[], []], [[], []], [[], []]
    early = {}

    for i in range(depth):
        if i % 2 == 0:
            a = i // 2
            lam_init = 0.8 - 0.6 * math.exp(-0.3 * i)
            wq, wk, wv = (_layer_bf16(w_qkv, a, s * d, d) for s in range(3))
            wo = _layer_bf16(w_o, a)
            lam_params = [p[a].reshape(1, dh) for p in (lambda_q1, lambda_k1, lambda_q2, lambda_k2)]
            for si, st in enumerate(sets):
                hn = _rmsnorm(st["h"], mixer_norm_g[i], _BF16)
                q = _matmul(hn, wq, out_dtypes=(_BF16,), name="q_proj")
                k32, k16 = _matmul(hn, wk, out_dtypes=(_F32, _BF16), lane_rows_first=True,
                                   tiles=(512, SUBLANES * LANES, d), name="k_proj")
                v32, v16 = _matmul(hn, wv, out_dtypes=(_F32, _BF16), name="v_proj")
                kw = dict(batch=st["batch"], seq=st["seq"], n_heads=n_heads, dh=dh, lam_init=lam_init)
                if st["prompt"]:
                    o = _attn_prompt(q, k16, v16, lam_params, subln_g[a], **kw)
                else:
                    o = _attn_sample(q, k16, v16, cache_k, cache_v, a, lam_params, subln_g[a], **kw)
                st["h"] = _matmul(o, wo, resid=st["h"], name="o_proj")
                new_k[si].append(k32.reshape(st["batch"], st["seq"], n_heads, 2, dh))
                new_v[si].append(v32.reshape(st["batch"], st["seq"], n_heads, 2 * dh))
        else:
            c = i // 2
            w1, w2 = early.pop(("w_pw1", c), None), early.pop(("w_pw2", c), None)
            if w1 is None:
                w1, w2 = _layer_bf16(w_pw1, c), _layer_bf16(w_pw2, c)
            for si, st in enumerate(sets):
                hn = _rmsnorm(st["h"], mixer_norm_g[i], _BF16)
                glu = _glu_matmul(hn, w1, b_pw1[c])
                state = None if st["prompt"] else state_conv[c]
                z = _conv_module(glu, state, w_dw[c], b_dw[c], conv_ln_g[c], conv_ln_b[c],
                                 batch=st["batch"], seq=st["seq"])
                st["h"] = _matmul(z, w2, bias=b_pw2[c], resid=st["h"], name="pw2")
                glu3 = glu.reshape(st["batch"], st["seq"], d)
                if st["prompt"]:
                    new_conv[si].append(glu3[:, seq - (CONV_WIDTH - 1):])
                else:
                    ctx = jnp.concatenate([state, glu3], axis=1)
                    new_conv[si].append(ctx[:, -(CONV_WIDTH - 1):])
        wpq = early.pop(("peer_wq", i), None)
        if wpq is None:
            wpq = _layer_bf16(peer_wq, i)
        keys = peer_sub_keys[i].astype(_BF16)
        jobs = [("peer_v", peer_v, i)]
        if i + 1 < depth:
            jobs.append(("peer_wq", peer_wq, i + 1))
            if (i + 1) % 2 == 1:
                jobs += [("w_pw1", w_pw1, (i + 1) // 2), ("w_pw2", w_pw2, (i + 1) // 2)]
        v_tab = None
        for st in sets:
            hn, gates = _peer_route_gates(st["h"], ffn_norm_g[i], wpq, keys)
            side = [(w, wl) for _, w, wl in jobs] if st["prompt"] else []
            weights, copies = _peer_weights(hn, peer_u, i, gates, side)
            if st["prompt"]:
                v_tab = copies[0]
                early.update({(name, wl): cp for (name, _, wl), cp in zip(jobs[1:], copies[1:])})
            st["h"] = _peer_mix(weights, v_tab, st["h"])

    y_prompt = _rmsnorm(sets[0]["h"], final_norm_g, _F32).reshape(batch, seq, d)
    y_sample = _rmsnorm(sets[1]["h"], final_norm_g, _F32).reshape(dbatch, dseq, d)
    return (y_prompt, y_sample, jnp.stack(new_k[0]), jnp.stack(new_v[0]), jnp.stack(new_conv[0]),
            jnp.stack(new_k[1]), jnp.stack(new_v[1]), jnp.stack(new_conv[1]))
```

```python
import functools
import math

import jax
import jax.numpy as jnp
from jax import lax
from jax.experimental import pallas as pl
from jax.experimental.pallas import tpu as pltpu

EPS = 1e-6
CHUNK = 64
CONV_WIDTH = 31
CONV_HALO = 32
PEER_TOPK = 16
LANES = 128
SUBLANES = 8
VMEM_LIMIT = 56 * 1024 * 1024

_NT = (((1,), (1,)), ((), ()))
_F32 = jnp.float32
_BF16 = jnp.bfloat16


def _params(*sem):
    return pltpu.CompilerParams(dimension_semantics=sem, vmem_limit_bytes=VMEM_LIMIT)


def _cast_kernel(x_ref, o_ref):
    o_ref[...] = x_ref[0].astype(o_ref.dtype)


def _layer_bf16(w, layer, col0=0, ncols=None):
    _, rows, cols = w.shape
    ncols = cols if ncols is None else ncols
    tr, tc = min(512, rows), min(2048, ncols)
    assert col0 % tc == 0
    c0 = col0 // tc
    return pl.pallas_call(
        _cast_kernel,
        grid=(rows // tr, ncols // tc),
        in_specs=[pl.BlockSpec((1, tr, tc), lambda i, j: (layer, i, j + c0))],
        out_specs=pl.BlockSpec((tr, tc), lambda i, j: (i, j)),
        out_shape=jax.ShapeDtypeStruct((rows, ncols), _BF16),
        compiler_params=_params("parallel", "parallel"),
        name="to_bf16",
    )(w)


def _rmsnorm_kernel(x_ref, g_ref, o_ref):
    x = x_ref[...]
    ms = jnp.mean(x * x, axis=-1, keepdims=True)
    o_ref[...] = (x * lax.rsqrt(ms + EPS) * g_ref[...]).astype(o_ref.dtype)


def _rmsnorm(x, g, out_dtype, tr=256):
    m, d = x.shape
    return pl.pallas_call(
        _rmsnorm_kernel,
        grid=(m // tr,),
        in_specs=[pl.BlockSpec((tr, d), lambda i: (i, 0)), pl.BlockSpec((1, d), lambda i: (0, 0))],
        out_specs=pl.BlockSpec((tr, d), lambda i: (i, 0)),
        out_shape=jax.ShapeDtypeStruct((m, d), out_dtype),
        compiler_params=_params("parallel"),
        name="rmsnorm",
    )(x, g.reshape(1, d))


def _side_specs(side, grid):
    steps = math.prod(grid)

    def step(*ids):
        s = 0
        for i, n in zip(ids, grid):
            s = s * n + i
        return s

    ins, outs, shapes = [], [], []
    for w, layer, col0, ncols in side:
        rows = w.shape[1]
        chunk = rows // steps
        assert chunk * steps == rows and chunk % (2 * SUBLANES) == 0 and col0 % ncols == 0
        ins.append(pl.BlockSpec((None, chunk, ncols),
                                lambda *ids, layer=layer, cb=col0 // ncols: (layer, step(*ids), cb)))
        outs.append(pl.BlockSpec((chunk, ncols), lambda *ids: (step(*ids), 0)))
        shapes.append(jax.ShapeDtypeStruct((rows, ncols), _BF16))
    return ins, outs, shapes


def _mm_kernel(*refs, has_bias, has_resid, n_out, n_side, multi_k):
    x_ref, w_ref = refs[0], refs[1]
    pos = 2
    b_ref = r_ref = None
    if has_bias:
        b_ref = refs[pos]
        pos += 1
    if has_resid:
        r_ref = refs[pos]
        pos += 1
    side_in = refs[pos:pos + n_side]
    pos += n_side
    out_refs = refs[pos:pos + n_out]
    for src, dst in zip(side_in, refs[pos + n_out:pos + n_out + n_side]):
        dst[...] = src[...].astype(dst.dtype)

    def finish(r):
        if has_bias:
            r = r + b_ref[...]
        if has_resid:
            r = r + r_ref[...]
        for o in out_refs:
            if len(o.shape) == 2:
                o[...] = r.astype(o.dtype)
            else:
                rows, groups, _ = o.shape
                flat = o.reshape(rows * groups, LANES)
                for c in range(groups):
                    flat[pl.ds(c, rows, stride=groups), :] = r[:, c * LANES:(c + 1) * LANES].astype(o.dtype)

    def product():
        return jnp.dot(x_ref[...], w_ref[...], preferred_element_type=_F32)

    if not multi_k:
        finish(product())
        return
    acc_ref = refs[pos + n_out + n_side]
    k = pl.program_id(2)

    @pl.when(k == 0)
    def _():
        acc_ref[...] = product()

    @pl.when(k > 0)
    def _():
        acc_ref[...] += product()

    @pl.when(k == pl.num_programs(2) - 1)
    def _():
        finish(acc_ref[...])


def _mm_tiles(m, n, k):
    return min(1024, m), min(512, n), min(4096, k)


def _matmul(x, w, *, bias=None, resid=None, out_dtypes=(_F32,), lane_rows_first=False, tiles=None, side=(),
            name="matmul"):
    m, kdim = x.shape
    n = w.shape[-1]
    tm, tn, tk = tiles or _mm_tiles(m, n, kdim)
    tm, tn, tk = min(tm, m), min(tn, n), min(tk, kdim)
    multi_k = kdim > tk
    out_specs = [pl.BlockSpec((tm, tn), lambda i, j, k: (i, j)) for _ in out_dtypes]
    out_shape = [jax.ShapeDtypeStruct((m, n), dt) for dt in out_dtypes]
    if lane_rows_first:
        assert (tn // LANES) % SUBLANES == 0 or tn == n
        out_specs[0] = pl.BlockSpec((tm, tn // LANES, LANES), lambda i, j, k: (i, j, 0))
        out_shape[0] = jax.ShapeDtypeStruct((m, n // LANES, LANES), out_dtypes[0])
    in_specs = [pl.BlockSpec((tm, tk), lambda i, j, k: (i, k)),
                pl.BlockSpec((tk, tn), lambda i, j, k: (k, j))]
    args = [x, w]
    if bias is not None:
        in_specs.append(pl.BlockSpec((1, tn), lambda i, j, k: (0, j)))
        args.append(bias.reshape(1, n))
    if resid is not None:
        in_specs.append(pl.BlockSpec((tm, tn), lambda i, j, k: (i, j)))
        args.append(resid)
    grid = (m // tm, n // tn, kdim // tk)
    side_in, side_out, side_shape = _side_specs(side, grid)
    outs = pl.pallas_call(
        functools.partial(_mm_kernel, has_bias=bias is not None, has_resid=resid is not None,
                          n_out=len(out_dtypes), n_side=len(side), multi_k=multi_k),
        grid=grid,
        in_specs=in_specs + side_in,
        out_specs=out_specs + side_out,
        out_shape=out_shape + side_shape,
        scratch_shapes=[pltpu.VMEM((tm, tn), _F32)] if multi_k else [],
        compiler_params=_params("parallel", "parallel", "arbitrary"),
        name=name,
    )(*args, *[s[0] for s in side])
    main = outs[:len(out_dtypes)]
    main = main if len(out_dtypes) > 1 else main[0]
    return (main, list(outs[len(out_dtypes):])) if side else main


def _glu_kernel(x_ref, wa_ref, wg_ref, ba_ref, bg_ref, o_ref):
    x = x_ref[...]
    a = jnp.dot(x, wa_ref[...], preferred_element_type=_F32) + ba_ref[...]
    gate = jnp.dot(x, wg_ref[...], preferred_element_type=_F32) + bg_ref[...]
    o_ref[...] = a * jax.nn.sigmoid(gate)


def _glu_matmul(x, w, b):
    m, kdim = x.shape
    n = w.shape[1] // 2
    tm, tn, tk = _mm_tiles(m, n, kdim)
    assert tk == kdim
    nj = n // tn
    b2 = b.reshape(1, 2 * n)
    return pl.pallas_call(
        _glu_kernel,
        grid=(m // tm, nj),
        in_specs=[pl.BlockSpec((tm, kdim), lambda i, j: (i, 0)),
                  pl.BlockSpec((kdim, tn), lambda i, j: (0, j)),
                  pl.BlockSpec((kdim, tn), lambda i, j: (0, j + nj)),
                  pl.BlockSpec((1, tn), lambda i, j: (0, j)),
                  pl.BlockSpec((1, tn), lambda i, j: (0, j + nj))],
        out_specs=pl.BlockSpec((tm, tn), lambda i, j: (i, j)),
        out_shape=jax.ShapeDtypeStruct((m, n), _F32),
        compiler_params=_params("parallel", "parallel"),
        name="glu_matmul",
    )(x, w, w, b2, b2)


def _lambda(lq1_ref, lk1_ref, lq2_ref, lk2_ref, lam_init):
    s1 = jnp.sum(lq1_ref[...] * lk1_ref[...], axis=-1, keepdims=True)
    s2 = jnp.sum(lq2_ref[...] * lk2_ref[...], axis=-1, keepdims=True)
    return jnp.exp(s1) - jnp.exp(s2) + lam_init


def _sub_layernorm(o, g_ref, lam_init):
    ms = jnp.mean(o * o, axis=-1, keepdims=True)
    return (o * lax.rsqrt(ms + EPS)) * g_ref[...] * (1.0 - lam_init)


def _attn_prompt_kernel(lq1_ref, lk1_ref, lq2_ref, lk2_ref, g_ref, q_ref, k_ref, v_ref, *rest, tq, dh, lam_init, qi):
    o_ref, s_ref, acc_ref = rest[-3:]
    scale = dh ** -0.5 * math.log2(math.e)
    q = q_ref[...]
    run_max = [None, None]
    for kj in range(qi + 1):
        k = k_ref[pl.ds(kj * tq, tq), :]
        for c in range(2):
            s = lax.dot_general(q[:, c * dh:(c + 1) * dh], k[:, c * dh:(c + 1) * dh], _NT,
                                preferred_element_type=_F32) * scale
            if kj == qi:
                row_chunk = lax.broadcasted_iota(jnp.int32, (tq, tq), 0) // CHUNK
                col_chunk = lax.broadcasted_iota(jnp.int32, (tq, tq), 1) // CHUNK
                s = jnp.where(col_chunk <= row_chunk, s, -jnp.inf)
            s_ref[pl.ds(c * tq, tq), pl.ds(kj * tq, tq)] = s
            part = s[:, :LANES]
            for j in range(1, tq // LANES):
                part = jnp.maximum(part, s[:, j * LANES:(j + 1) * LANES])
            run_max[c] = part if run_max[c] is None else jnp.maximum(run_max[c], part)
    m = jnp.concatenate(run_max, axis=0)
    row_max = jnp.broadcast_to(jnp.max(m, axis=-1, keepdims=True), m.shape)
    total = None
    for kj in range(qi + 1):
        ps = []
        for j in range(tq // LANES):
            p = jnp.exp2(s_ref[:, pl.ds(kj * tq + j * LANES, LANES)] - row_max)
            total = p if total is None else total + p
            ps.append(p.astype(_BF16))
        contrib = jnp.dot(jnp.concatenate(ps, axis=1), v_ref[pl.ds(kj * tq, tq), :], preferred_element_type=_F32)
        if kj == 0:
            acc_ref[...] = contrib
        else:
            acc_ref[...] += contrib
    lam = _lambda(lq1_ref, lk1_ref, lq2_ref, lk2_ref, lam_init)
    o = acc_ref[...] / jnp.sum(total, axis=-1, keepdims=True)
    o = o[:tq] - lam * o[tq:]
    o_ref[...] = _sub_layernorm(o, g_ref, lam_init).astype(o_ref.dtype)


def _attn_prompt(q, k, v, lam_params, subln_g, *, batch, seq, n_heads, dh, lam_init, tq=512):
    hw = 2 * dh
    assert seq % tq == 0 and tq % CHUNK == 0
    nq = seq // tq
    small = pl.BlockSpec((1, dh), lambda b, h: (0, 0))
    out = None
    for qi in range(nq):
        tile = pl.BlockSpec((tq, hw), lambda b, h, qi=qi: (b * nq + qi, h))
        in_specs = [small, small, small, small,
                    pl.BlockSpec((1, hw), lambda b, h: (0, 0)),
                    tile,
                    pl.BlockSpec((seq, hw), lambda b, h: (b, h)),
                    pl.BlockSpec((seq, hw), lambda b, h: (b, h))]
        args = [*lam_params, subln_g.reshape(1, hw), q, k, v]
        if out is not None:
            in_specs.append(pl.BlockSpec(memory_space=pl.ANY))
            args.append(out)
        out = pl.pallas_call(
            functools.partial(_attn_prompt_kernel, tq=tq, dh=dh, lam_init=lam_init, qi=qi),
            grid=(batch, n_heads),
            in_specs=in_specs,
            out_specs=tile,
            out_shape=jax.ShapeDtypeStruct(q.shape, _BF16),
            scratch_shapes=[pltpu.VMEM((2 * tq, (qi + 1) * tq), _F32), pltpu.VMEM((2 * tq, hw), _F32)],
            input_output_aliases={len(args) - 1: 0} if qi else {},
            compiler_params=_params("parallel", "parallel"),
            name="attn_prompt",
        )(*args)
    return out


def _attn_sample_kernel(lq1_ref, lk1_ref, lq2_ref, lk2_ref, g_ref, q_ref, kn_ref, vn_ref, ka_ref, kb_ref,
                        vlo_ref, vhi_ref, o_ref, *, dh, lam_init):
    scale = dh ** -0.5
    past = ka_ref.shape[0]
    hw = 2 * dh
    k_rows = [r.reshape(past * SUBLANES, dh) for r in (ka_ref, kb_ref)]
    v_rows = [r.reshape(past * SUBLANES, dh) for r in (vlo_ref, vhi_ref)]
    lam = _lambda(lq1_ref, lk1_ref, lq2_ref, lk2_ref, lam_init)
    per_block = SUBLANES // 2
    for hl in range(SUBLANES):
        cols = slice(hl * hw, (hl + 1) * hw)
        q = q_ref[:, cols]
        kn = kn_ref[:, cols]
        a_p = a_n = None
        for c in range(2):
            row = (hl % per_block) * 2 + c
            kp = k_rows[hl // per_block][pl.ds(row, past, stride=SUBLANES), :].astype(_BF16)
            qc = q[:, c * dh:(c + 1) * dh]
            s_p = lax.dot_general(qc, kp, _NT, preferred_element_type=_F32) * scale
            s_n = lax.dot_general(qc, kn[:, c * dh:(c + 1) * dh], _NT, preferred_element_type=_F32) * scale
            m = jnp.maximum(jnp.max(s_p, axis=-1, keepdims=True), jnp.max(s_n, axis=-1, keepdims=True))
            e_p = jnp.exp(s_p - m)
            e_n = jnp.exp(s_n - m)
            denom = jnp.sum(e_p, axis=-1, keepdims=True) + jnp.sum(e_n, axis=-1, keepdims=True)
            p_p = e_p / denom
            p_n = e_n / denom
            if c == 0:
                a_p, a_n = p_p, p_n
            else:
                a_p, a_n = a_p - lam * p_p, a_n - lam * p_n
        vp = jnp.concatenate([v[pl.ds(hl, past, stride=SUBLANES), :] for v in v_rows], axis=1).astype(_BF16)
        o = (jnp.dot(a_p.astype(_BF16), vp, preferred_element_type=_F32)
             + jnp.dot(a_n.astype(_BF16), vn_ref[:, cols], preferred_element_type=_F32))
        o_ref[:, cols] = _sub_layernorm(o, g_ref, lam_init).astype(o_ref.dtype)


def _attn_sample(q, k, v, cache_k, cache_v, layer, lam_params, subln_g, *, batch, seq, n_heads, dh, lam_init):
    hw = 2 * dh
    past = cache_k.shape[2]
    assert n_heads % SUBLANES == 0 and dh == LANES
    gw = SUBLANES * hw
    ck = cache_k.reshape(-1, past, n_heads * 2, dh)
    cv = cache_v.reshape(-1, past, n_heads, hw)
    base = layer * batch
    small = pl.BlockSpec((1, dh), lambda b, g: (0, 0))
    new = pl.BlockSpec((seq, gw), lambda b, g: (b, g))
    old = lambda sub, lane: pl.BlockSpec((None, past, SUBLANES, dh), lambda b, g: (base + b, 0, sub(g), lane))
    return pl.pallas_call(
        functools.partial(_attn_sample_kernel, dh=dh, lam_init=lam_init),
        grid=(batch, n_heads // SUBLANES),
        in_specs=[small, small, small, small, pl.BlockSpec((1, hw), lambda b, g: (0, 0)),
                  new, new, new,
                  old(lambda g: 2 * g, 0), old(lambda g: 2 * g + 1, 0), old(lambda g: g, 0), old(lambda g: g, 1)],
        out_specs=new,
        out_shape=jax.ShapeDtypeStruct(q.shape, _BF16),
        compiler_params=_params("parallel", "parallel"),
        name="attn_sample",
    )(*lam_params, subln_g.reshape(1, hw), q, k, v, ck, ck, cv, cv)


def _conv_ln_silu(ext_ref, c_ref, w_ref, bdw_ref, g_ref, b_ref, o_ref, *, rows):
    nslab = c_ref.shape[0]
    d = nslab * LANES
    first = CONV_HALO - (CONV_WIDTH - 1)

    def slab(si, carry):
        lanes = pl.ds(pl.multiple_of(si * LANES, LANES), LANES)
        taps = [jnp.broadcast_to(w_ref[pl.ds(j, 1), lanes], (SUBLANES, LANES)) for j in range(CONV_WIDTH)]
        bias = jnp.broadcast_to(bdw_ref[:, lanes], (SUBLANES, LANES))
        for r0 in range(0, rows, 2 * SUBLANES):
            for parity in range(2):
                acc = bias
                for j in range(CONV_WIDTH):
                    acc = acc + taps[j] * ext_ref[si, pl.ds(first + r0 + parity + j, SUBLANES, stride=2), :]
                c_ref[si, pl.ds(r0 + parity, SUBLANES, stride=2), :] = acc
        return carry

    lax.fori_loop(0, nslab, slab, 0, unroll=2)

    total = c_ref[0]
    for si in range(1, nslab):
        total = total + c_ref[si]
    mu = jnp.broadcast_to(jnp.sum(total, axis=-1, keepdims=True) * (1.0 / d), (rows, LANES))
    total = None
    for si in range(nslab):
        cc = c_ref[si] - mu
        total = cc * cc if total is None else total + cc * cc
    var = jnp.sum(total, axis=-1, keepdims=True) * (1.0 / d)
    inv = jnp.broadcast_to(lax.rsqrt(var + EPS), (rows, LANES))
    for si in range(nslab):
        lanes = slice(si * LANES, (si + 1) * LANES)
        y = (c_ref[si] - mu) * inv * g_ref[:, lanes] + b_ref[:, lanes]
        o_ref[:, lanes] = (y * jax.nn.sigmoid(y)).astype(o_ref.dtype)


def _conv_prompt_kernel(x_ref, halo_ref, w_ref, bdw_ref, g_ref, b_ref, o_ref, ext_ref, c_ref, *, rows):
    i = pl.program_id(1)
    for si in range(ext_ref.shape[0]):
        lanes = slice(si * LANES, (si + 1) * LANES)
        halo = halo_ref[:, lanes]
        ext_ref[si, pl.ds(0, CONV_HALO), :] = jnp.where(i == 0, jnp.zeros_like(halo), halo)
        ext_ref[si, pl.ds(CONV_HALO, rows), :] = x_ref[:, lanes]
    _conv_ln_silu(ext_ref, c_ref, w_ref, bdw_ref, g_ref, b_ref, o_ref, rows=rows)


def _conv_sample_kernel(x_ref, st_ref, w_ref, bdw_ref, g_ref, b_ref, o_ref, ext_ref, c_ref, *, rows):
    ctx = CONV_WIDTH - 1
    for si in range(ext_ref.shape[0]):
        lanes = slice(si * LANES, (si + 1) * LANES)
        ext_ref[si, pl.ds(CONV_HALO - ctx, ctx), :] = st_ref[0, :, lanes]
        ext_ref[si, pl.ds(CONV_HALO, rows), :] = x_ref[:, lanes]
    _conv_ln_silu(ext_ref, c_ref, w_ref, bdw_ref, g_ref, b_ref, o_ref, rows=rows)


def _conv_module(x, state, w_dw, b_dw, ln_g, ln_b, *, batch, seq):
    d = x.shape[1]
    vec = lambda a: a.reshape(1, d)
    if state is None:
        rows = 128
        nt = seq // rows
        grid = (batch, nt)
        per = rows // CONV_HALO
        const = lambda b, i: (0, 0)
        in_specs = [pl.BlockSpec((rows, d), lambda b, i: (b * nt + i, 0)),
                    pl.BlockSpec((CONV_HALO, d), lambda b, i: (jnp.maximum((b * nt + i) * per - 1, 0), 0))]
        out_spec = pl.BlockSpec((rows, d), lambda b, i: (b * nt + i, 0))
        body = functools.partial(_conv_prompt_kernel, rows=rows)
        args = [x, x]
        sem = ("parallel", "arbitrary")
        name = "conv_prompt"
    else:
        rows = seq
        grid = (batch,)
        const = lambda b: (0, 0)
        in_specs = [pl.BlockSpec((rows, d), lambda b: (b, 0)),
                    pl.BlockSpec((1, CONV_WIDTH - 1, d), lambda b: (b, 0, 0))]
        out_spec = pl.BlockSpec((rows, d), lambda b: (b, 0))
        body = functools.partial(_conv_sample_kernel, rows=rows)
        args = [x, state]
        sem = ("parallel",)
        name = "conv_sample"
    in_specs += [pl.BlockSpec((CONV_WIDTH, d), const)] + [pl.BlockSpec((1, d), const)] * 3
    return pl.pallas_call(
        body,
        grid=grid,
        in_specs=in_specs,
        out_specs=out_spec,
        out_shape=jax.ShapeDtypeStruct(x.shape, _BF16),
        scratch_shapes=[pltpu.VMEM((d // LANES, CONV_HALO + rows, LANES), _F32),
                        pltpu.VMEM((d // LANES, rows, LANES), _F32)],
        compiler_params=_params(*sem),
        name=name,
    )(*args, w_dw, vec(b_dw), vec(ln_g), vec(ln_b))


def _top_rows(s, k):
    n = s.shape[0]
    iota = lax.broadcasted_iota(jnp.int32, s.shape, 0).astype(_F32)
    vals, idxs = [], []
    for _ in range(k):
        m = jnp.max(s, axis=0, keepdims=True)
        am = jnp.min(jnp.where(s == m, iota, float(n)), axis=0, keepdims=True)
        vals.append(m)
        idxs.append(am)
        s = jnp.where(iota == am, -jnp.inf, s)
    return jnp.concatenate(vals, axis=0), jnp.concatenate(idxs, axis=0)


def _pick_row(rank, table):
    out = jnp.zeros_like(rank)
    for k in range(table.shape[0]):
        out = jnp.where(rank == float(k), table[k:k + 1], out)
    return out


def _router_kernel(q_ref, keys_ref, a_ref, b_ref, g_ref, *, half):
    kk = PEER_TOPK
    lim = 4
    assert lim * lim >= kk
    q = q_ref[...]
    tops = []
    for c in range(2):
        s = lax.dot_general(keys_ref[0, c], q[:, c * half:(c + 1) * half], _NT, preferred_element_type=_F32)
        tops.append(_top_rows(s, kk))
    (s0, i0), (s1, i1) = tops
    rank = lax.broadcasted_iota(jnp.int32, s0.shape, 0)
    rank_f = rank.astype(_F32)
    never = float(kk * kk)
    cands, flats = [], []
    for k1 in range(lim):
        ok = rank < kk // (k1 + 1)
        cands.append(jnp.where(ok, s0[k1:k1 + 1] + s1, -jnp.inf))
        flats.append(jnp.where(ok, k1 * kk + rank_f, never))
    for k2 in range(lim):
        if kk // (k2 + 1) <= lim:
            continue
        ok = (rank >= lim) & (rank < kk // (k2 + 1))
        cands.append(jnp.where(ok, s0 + s1[k2:k2 + 1], -jnp.inf))
        flats.append(jnp.where(ok, rank_f * kk + k2, never))
    cand = jnp.concatenate(cands, axis=0)
    flat = jnp.concatenate(flats, axis=0)
    best, pos = [], []
    for _ in range(kk):
        m = jnp.max(cand, axis=0, keepdims=True)
        p = jnp.min(jnp.where(cand == m, flat, never), axis=0, keepdims=True)
        best.append(m)
        pos.append(p)
        cand = jnp.where(flat == p, -jnp.inf, cand)
    best = jnp.concatenate(best, axis=0)
    pos = jnp.concatenate(pos, axis=0)
    k1 = jnp.floor(pos * (1.0 / kk))
    k2 = pos - k1 * kk
    e = jnp.exp(best - best[0:1])
    g_ref[...] = e / jnp.sum(e, axis=0, keepdims=True)
    a_ref[...] = _pick_row(k1, i0).astype(jnp.int32)
    b_ref[...] = _pick_row(k2, i1).astype(jnp.int32)


def _peer_route(q, sub_keys, *, tm=1024):
    n = q.shape[0]
    tm = min(tm, n)
    heads, _, n_keys, half = sub_keys.shape
    nsel = heads * PEER_TOPK
    out_spec = pl.BlockSpec((PEER_TOPK, tm), lambda i, h: (h, i))
    return pl.pallas_call(
        functools.partial(_router_kernel, half=half),
        grid=(n // tm, heads),
        in_specs=[pl.BlockSpec((tm, 2 * half), lambda i, h: (i, h)),
                  pl.BlockSpec((1, 2, n_keys, half), lambda i, h: (h, 0, 0, 0))],
        out_specs=[out_spec, out_spec, out_spec],
        out_shape=[jax.ShapeDtypeStruct((nsel, n), jnp.int32), jax.ShapeDtypeStruct((nsel, n), jnp.int32),
                   jax.ShapeDtypeStruct((nsel, n), _F32)],
        compiler_params=_params("parallel", "parallel"),
        name="peer_route",
    )(q, sub_keys)


def _gate_matrix_kernel(a_ref, b_ref, g_ref, o_ref, at_ref, bt_ref, gt_ref, s_ref, *, n_keys, tg):
    at_ref[...] = a_ref[...].T
    bt_ref[...] = b_ref[...].T
    gt_ref[...] = g_ref[...].T
    nsel = at_ref.shape[1]
    pitch = _gate_pitch(n_keys)
    half = SUBLANES // 2
    key_id = lax.broadcasted_iota(jnp.int32, (n_keys, nsel), 0)
    key_id_odd = lax.broadcasted_iota(jnp.int32, (n_keys + SUBLANES, nsel), 0) - half

    def grid_of(t, ids):
        row = pl.ds(t, 1)
        lhs = jnp.where(at_ref[row, :] == ids, gt_ref[row, :], 0.0).astype(_BF16)
        rhs = jnp.where(bt_ref[row, :] == key_id, 1.0, 0.0).astype(_BF16)
        return lax.dot_general(lhs, rhs, _NT, preferred_element_type=_F32)

    def token_pair(p, carry):
        base = pl.multiple_of(p * (2 * pitch), SUBLANES)
        s_ref[pl.ds(base, n_keys), :] = grid_of(2 * p, key_id)
        odd = pl.multiple_of(base + pitch - half, SUBLANES)
        s_ref[pl.ds(odd, n_keys + SUBLANES), :] = grid_of(2 * p + 1, key_id_odd)
        return carry

    lax.fori_loop(0, tg // 2, token_pair, 0, unroll=16)
    for i in range(n_keys):
        o_ref[:, i * n_keys:(i + 1) * n_keys] = s_ref[pl.ds(i, tg, stride=pitch), :].astype(o_ref.dtype)


def _gate_pitch(n_keys):
    assert n_keys % SUBLANES == 0
    return n_keys + SUBLANES // 2


def _gate_matrix(sel_a, sel_b, gate, *, n_keys, tg=128):
    nsel, n = sel_a.shape
    in_spec = pl.BlockSpec((nsel, tg), lambda i: (0, i))
    return pl.pallas_call(
        functools.partial(_gate_matrix_kernel, n_keys=n_keys, tg=tg),
        grid=(n // tg,),
        in_specs=[in_spec, in_spec, in_spec],
        out_specs=pl.BlockSpec((tg, n_keys * n_keys), lambda i: (i, 0)),
        out_shape=jax.ShapeDtypeStruct((n, n_keys * n_keys), _BF16),
        scratch_shapes=[pltpu.VMEM((tg, nsel), jnp.int32), pltpu.VMEM((tg, nsel), jnp.int32),
                        pltpu.VMEM((tg, nsel), _F32), pltpu.VMEM((tg * _gate_pitch(n_keys), n_keys), _F32)],
        compiler_params=_params("parallel"),
        name="peer_gates",
    )(sel_a, sel_b, gate)


def _peer_weights_kernel(x_ref, u_ref, gates_ref, *rest):
    n_side = (len(rest) - 1) // 2
    o_ref = rest[n_side]
    act = lax.dot_general(x_ref[...], u_ref[...].astype(x_ref.dtype), _NT, preferred_element_type=_F32)
    gelu = 0.5 * act * (1.0 + lax.erf(act * (2.0 ** -0.5)))
    o_ref[...] = (gelu * gates_ref[...].astype(_F32)).astype(o_ref.dtype)
    for src, dst in zip(rest[:n_side], rest[n_side + 1:]):
        dst[...] = src[...].astype(dst.dtype)


def _peer_weights(x, u, layer, gates, side=(), *, tm=1024, te=512):
    n, d = x.shape
    n_exp = u.shape[1]
    tm = min(tm, n)
    grid = (n // tm, n_exp // te)
    side_in, side_out, side_shape = _side_specs(side, grid)
    outs = pl.pallas_call(
        _peer_weights_kernel,
        grid=grid,
        in_specs=[pl.BlockSpec((tm, d), lambda i, j: (i, 0)),
                  pl.BlockSpec((None, te, d), lambda i, j: (layer, j, 0)),
                  pl.BlockSpec((tm, te), lambda i, j: (i, j))] + side_in,
        out_specs=[pl.BlockSpec((tm, te), lambda i, j: (i, j))] + side_out,
        out_shape=[jax.ShapeDtypeStruct((n, n_exp), _BF16)] + side_shape,
        compiler_params=_params("arbitrary", "arbitrary"),
        name="peer_weights",
    )(x, u, gates, *[s[0] for s in side])
    return outs[0], list(outs[1:])


def _peer_route_gates(h, norm_g, w_q, sub_keys):
    assert h.shape[0] % LANES == 0
    hn = _rmsnorm(h, norm_g, _BF16)
    q = _matmul(hn, w_q, out_dtypes=(_BF16,), name="peer_query")
    sel_a, sel_b, gate = _peer_route(q, sub_keys)
    return hn, _gate_matrix(sel_a, sel_b, gate, n_keys=sub_keys.shape[2])


def _peer_mix(weights, v_tab, h):
    return _matmul(weights, v_tab, resid=h, tiles=(1024, 1024, 2048), name="peer_mix")


def kernel(x_prompt, x_sample, cache_k, cache_v, state_conv, mixer_norm_g, ffn_norm_g, final_norm_g, w_qkv, lambda_q1, lambda_k1, lambda_q2, lambda_k2, subln_g, w_o, w_pw1, b_pw1, w_dw, b_dw, conv_ln_g, conv_ln_b, w_pw2, b_pw2, peer_wq, peer_sub_keys, peer_u, peer_v):
    batch, seq, d = x_prompt.shape
    dbatch, dseq, _ = x_sample.shape
    depth = mixer_norm_g.shape[0]
    n_heads, dh = cache_k.shape[3], cache_k.shape[5]
    past = cache_k.shape[2]
    assert dseq == CHUNK and past % CHUNK == 0
    sets = [dict(h=x_prompt.reshape(batch * seq, d), batch=batch, seq=seq, prompt=True),
            dict(h=x_sample.reshape(dbatch * dseq, d), batch=dbatch, seq=dseq, prompt=False)]
    new_k, new_v, new_conv = [[], []], [[], []], [[], []]
    early = {}

    for i in range(depth):
        if i % 2 == 0:
            a = i // 2
            lam_init = 0.8 - 0.6 * math.exp(-0.3 * i)
            wq = _layer_bf16(w_qkv, a, 0, d)
            wk = wv = wo = None
            lam_params = [p[a].reshape(1, dh) for p in (lambda_q1, lambda_k1, lambda_q2, lambda_k2)]
            for si, st in enumerate(sets):
                first = st["prompt"]
                hn = _rmsnorm(st["h"], mixer_norm_g[i], _BF16)
                q = _matmul(hn, wq, out_dtypes=(_BF16,), side=[(w_qkv, a, d, d)] if first else (), name="q_proj")
                if first:
                    q, (wk,) = q
                kk = _matmul(hn, wk, out_dtypes=(_F32, _BF16), lane_rows_first=True,
                             tiles=(512, SUBLANES * LANES, d), side=[(w_qkv, a, 2 * d, d)] if first else (),
                             name="k_proj")
                if first:
                    kk, (wv,) = kk
                k32, k16 = kk
                vv = _matmul(hn, wv, out_dtypes=(_F32, _BF16), side=[(w_o, a, 0, d)] if first else (), name="v_proj")
                if first:
                    vv, (wo,) = vv
                v32, v16 = vv
                kw = dict(batch=st["batch"], seq=st["seq"], n_heads=n_heads, dh=dh, lam_init=lam_init)
                if st["prompt"]:
                    o = _attn_prompt(q, k16, v16, lam_params, subln_g[a], **kw)
                else:
                    o = _attn_sample(q, k16, v16, cache_k, cache_v, a, lam_params, subln_g[a], **kw)
                h_new = _matmul(o, wo, resid=st["h"], side=[(peer_wq, i, 0, peer_wq.shape[2])] if first else (),
                                name="o_proj")
                if first:
                    h_new, (early[("peer_wq", i)],) = h_new
                st["h"] = h_new
                new_k[si].append(k32.reshape(st["batch"], st["seq"], n_heads, 2, dh))
                new_v[si].append(v32.reshape(st["batch"], st["seq"], n_heads, 2 * dh))
        else:
            c = i // 2
            w1, w2 = early.pop(("w_pw1", c), None), early.pop(("w_pw2", c), None)
            if w1 is None:
                w1, w2 = _layer_bf16(w_pw1, c), _layer_bf16(w_pw2, c)
            for si, st in enumerate(sets):
                hn = _rmsnorm(st["h"], mixer_norm_g[i], _BF16)
                glu = _glu_matmul(hn, w1, b_pw1[c])
                state = None if st["prompt"] else state_conv[c]
                z = _conv_module(glu, state, w_dw[c], b_dw[c], conv_ln_g[c], conv_ln_b[c],
                                 batch=st["batch"], seq=st["seq"])
                st["h"] = _matmul(z, w2, bias=b_pw2[c], resid=st["h"], name="pw2")
                glu3 = glu.reshape(st["batch"], st["seq"], d)
                if st["prompt"]:
                    new_conv[si].append(glu3[:, seq - (CONV_WIDTH - 1):])
                else:
                    ctx = jnp.concatenate([state, glu3], axis=1)
                    new_conv[si].append(ctx[:, -(CONV_WIDTH - 1):])
        wpq = early.pop(("peer_wq", i), None)
        if wpq is None:
            wpq = _layer_bf16(peer_wq, i)
        keys = peer_sub_keys[i].astype(_BF16)
        jobs = [("peer_v", peer_v, i)]
        if i + 1 < depth and (i + 1) % 2 == 1:
            jobs += [("peer_wq", peer_wq, i + 1), ("w_pw1", w_pw1, (i + 1) // 2), ("w_pw2", w_pw2, (i + 1) // 2)]
        v_tab = None
        for st in sets:
            hn, gates = _peer_route_gates(st["h"], ffn_norm_g[i], wpq, keys)
            side = [(w, wl, 0, w.shape[2]) for _, w, wl in jobs] if st["prompt"] else []
            weights, copies = _peer_weights(hn, peer_u, i, gates, side)
            if st["prompt"]:
                v_tab = copies[0]
                early.update({(name, wl): cp for (name, _, wl), cp in zip(jobs[1:], copies[1:])})
            st["h"] = _peer_mix(weights, v_tab, st["h"])

    y_prompt = _rmsnorm(sets[0]["h"], final_norm_g, _F32).reshape(batch, seq, d)
    y_sample = _rmsnorm(sets[1]["h"], final_norm_g, _F32).reshape(dbatch, dseq, d)
    return (y_prompt, y_sample, jnp.stack(new_k[0]), jnp.stack(new_v[0]), jnp.stack(new_conv[0]),
            jnp.stack(new_k[1]), jnp.stack(new_v[1]), jnp.stack(new_conv[1]))
```

```python
import functools
import math

import jax
import jax.numpy as jnp
from jax import lax
from jax.experimental import pallas as pl
from jax.experimental.pallas import tpu as pltpu

EPS = 1e-6
CHUNK = 64
CONV_WIDTH = 31
CONV_HALO = 32
PEER_TOPK = 16
LANES = 128
SUBLANES = 8
VMEM_LIMIT = 56 * 1024 * 1024

_NT = (((1,), (1,)), ((), ()))
_F32 = jnp.float32
_BF16 = jnp.bfloat16


def _params(*sem):
    return pltpu.CompilerParams(dimension_semantics=sem, vmem_limit_bytes=VMEM_LIMIT)


def _streamed_call(body, *, grid, in_specs, out_specs, out_shape, name):
    def outer(*refs):
        pltpu.emit_pipeline(body, grid=grid, in_specs=in_specs, out_specs=out_specs)(*refs)

    hbm = pl.BlockSpec(memory_space=pl.ANY)
    return pl.pallas_call(
        outer,
        in_specs=[hbm] * len(in_specs),
        out_specs=[hbm] * len(out_specs),
        out_shape=out_shape,
        compiler_params=pltpu.CompilerParams(vmem_limit_bytes=VMEM_LIMIT),
        name=name,
    )


WEIGHT_BUFFERS = pl.Buffered(3)


def _cast_kernel(x_ref, o_ref):
    o_ref[...] = x_ref[0].astype(o_ref.dtype)


def _layer_bf16(w, layer, col0=0, ncols=None):
    _, rows, cols = w.shape
    ncols = cols if ncols is None else ncols
    tr, tc = min(512, rows), min(2048, ncols)
    assert col0 % tc == 0
    c0 = col0 // tc
    return pl.pallas_call(
        _cast_kernel,
        grid=(rows // tr, ncols // tc),
        in_specs=[pl.BlockSpec((1, tr, tc), lambda i, j: (layer, i, j + c0))],
        out_specs=pl.BlockSpec((tr, tc), lambda i, j: (i, j)),
        out_shape=jax.ShapeDtypeStruct((rows, ncols), _BF16),
        compiler_params=_params("parallel", "parallel"),
        name="to_bf16",
    )(w)


def _rmsnorm_kernel(x_ref, g_ref, o_ref):
    x = x_ref[...]
    ms = jnp.mean(x * x, axis=-1, keepdims=True)
    o_ref[...] = (x * lax.rsqrt(ms + EPS) * g_ref[...]).astype(o_ref.dtype)


def _rmsnorm(x, g, out_dtype, tr=256):
    m, d = x.shape
    return pl.pallas_call(
        _rmsnorm_kernel,
        grid=(m // tr,),
        in_specs=[pl.BlockSpec((tr, d), lambda i: (i, 0)), pl.BlockSpec((1, d), lambda i: (0, 0))],
        out_specs=pl.BlockSpec((tr, d), lambda i: (i, 0)),
        out_shape=jax.ShapeDtypeStruct((m, d), out_dtype),
        compiler_params=_params("parallel"),
        name="rmsnorm",
    )(x, g.reshape(1, d))


def _side_specs(side, grid):
    steps = math.prod(grid)

    def step(*ids):
        s = 0
        for i, n in zip(ids, grid):
            s = s * n + i
        return s

    ins, outs, shapes = [], [], []
    for w, layer, col0, ncols in side:
        rows = w.shape[1]
        chunk = rows // steps
        assert chunk * steps == rows and chunk % (2 * SUBLANES) == 0 and col0 % ncols == 0
        ins.append(pl.BlockSpec((None, chunk, ncols),
                                lambda *ids, layer=layer, cb=col0 // ncols: (layer, step(*ids), cb)))
        outs.append(pl.BlockSpec((chunk, ncols), lambda *ids: (step(*ids), 0)))
        shapes.append(jax.ShapeDtypeStruct((rows, ncols), _BF16))
    return ins, outs, shapes


def _mm_kernel(*refs, has_bias, has_resid, n_out, n_side, multi_k):
    x_ref, w_ref = refs[0], refs[1]
    pos = 2
    b_ref = r_ref = None
    if has_bias:
        b_ref = refs[pos]
        pos += 1
    if has_resid:
        r_ref = refs[pos]
        pos += 1
    side_in = refs[pos:pos + n_side]
    pos += n_side
    out_refs = refs[pos:pos + n_out]
    for src, dst in zip(side_in, refs[pos + n_out:pos + n_out + n_side]):
        dst[...] = src[...].astype(dst.dtype)

    def finish(r):
        if has_bias:
            r = r + b_ref[...]
        if has_resid:
            r = r + r_ref[...]
        for o in out_refs:
            if len(o.shape) == 2:
                o[...] = r.astype(o.dtype)
            else:
                rows, groups, _ = o.shape
                flat = o.reshape(rows * groups, LANES)
                for c in range(groups):
                    flat[pl.ds(c, rows, stride=groups), :] = r[:, c * LANES:(c + 1) * LANES].astype(o.dtype)

    def product():
        return jnp.dot(x_ref[...], w_ref[...], preferred_element_type=_F32)

    if not multi_k:
        finish(product())
        return
    acc_ref = refs[pos + n_out + n_side]
    k = pl.program_id(2)

    @pl.when(k == 0)
    def _():
        acc_ref[...] = product()

    @pl.when(k > 0)
    def _():
        acc_ref[...] += product()

    @pl.when(k == pl.num_programs(2) - 1)
    def _():
        finish(acc_ref[...])


def _mm_tiles(m, n, k):
    return min(1024, m), min(512, n), min(4096, k)


def _matmul(x, w, *, bias=None, resid=None, out_dtypes=(_F32,), lane_rows_first=False, tiles=None, side=(),
            name="matmul"):
    m, kdim = x.shape
    n = w.shape[-1]
    tm, tn, tk = tiles or _mm_tiles(m, n, kdim)
    tm, tn, tk = min(tm, m), min(tn, n), min(tk, kdim)
    multi_k = kdim > tk
    out_specs = [pl.BlockSpec((tm, tn), lambda i, j, k: (i, j)) for _ in out_dtypes]
    out_shape = [jax.ShapeDtypeStruct((m, n), dt) for dt in out_dtypes]
    if lane_rows_first:
        assert (tn // LANES) % SUBLANES == 0 or tn == n
        out_specs[0] = pl.BlockSpec((tm, tn // LANES, LANES), lambda i, j, k: (i, j, 0))
        out_shape[0] = jax.ShapeDtypeStruct((m, n // LANES, LANES), out_dtypes[0])
    in_specs = [pl.BlockSpec((tm, tk), lambda i, j, k: (i, k)),
                pl.BlockSpec((tk, tn), lambda i, j, k: (k, j))]
    args = [x, w]
    if bias is not None:
        in_specs.append(pl.BlockSpec((1, tn), lambda i, j, k: (0, j)))
        args.append(bias.reshape(1, n))
    if resid is not None:
        in_specs.append(pl.BlockSpec((tm, tn), lambda i, j, k: (i, j)))
        args.append(resid)
    grid = (m // tm, n // tn, kdim // tk)
    side_in, side_out, side_shape = _side_specs(side, grid)
    body = functools.partial(_mm_kernel, has_bias=bias is not None, has_resid=resid is not None,
                             n_out=len(out_dtypes), n_side=len(side), multi_k=multi_k)
    if multi_k:
        call = pl.pallas_call(
            body,
            grid=grid,
            in_specs=in_specs + side_in,
            out_specs=out_specs + side_out,
            out_shape=out_shape + side_shape,
            scratch_shapes=[pltpu.VMEM((tm, tn), _F32)],
            compiler_params=_params("parallel", "parallel", "arbitrary"),
            name=name,
        )
    else:
        in_specs[1] = pl.BlockSpec((tk, tn), lambda i, j, k: (k, j), pipeline_mode=WEIGHT_BUFFERS)
        call = _streamed_call(body, grid=grid, in_specs=in_specs + side_in, out_specs=out_specs + side_out,
                              out_shape=out_shape + side_shape, name=name)
    outs = call(*args, *[s[0] for s in side])
    main = outs[:len(out_dtypes)]
    main = main if len(out_dtypes) > 1 else main[0]
    return (main, list(outs[len(out_dtypes):])) if side else main


def _glu_kernel(x_ref, wa_ref, wg_ref, ba_ref, bg_ref, o_ref):
    x = x_ref[...]
    a = jnp.dot(x, wa_ref[...], preferred_element_type=_F32) + ba_ref[...]
    gate = jnp.dot(x, wg_ref[...], preferred_element_type=_F32) + bg_ref[...]
    o_ref[...] = a * jax.nn.sigmoid(gate)


def _glu_matmul(x, w, b):
    m, kdim = x.shape
    n = w.shape[1] // 2
    tm, tn, tk = _mm_tiles(m, n, kdim)
    assert tk == kdim
    nj = n // tn
    b2 = b.reshape(1, 2 * n)
    return _streamed_call(
        _glu_kernel,
        grid=(m // tm, nj),
        in_specs=[pl.BlockSpec((tm, kdim), lambda i, j: (i, 0)),
                  pl.BlockSpec((kdim, tn), lambda i, j: (0, j), pipeline_mode=WEIGHT_BUFFERS),
                  pl.BlockSpec((kdim, tn), lambda i, j: (0, j + nj), pipeline_mode=WEIGHT_BUFFERS),
                  pl.BlockSpec((1, tn), lambda i, j: (0, j)),
                  pl.BlockSpec((1, tn), lambda i, j: (0, j + nj))],
        out_specs=[pl.BlockSpec((tm, tn), lambda i, j: (i, j))],
        out_shape=[jax.ShapeDtypeStruct((m, n), _F32)],
        name="glu_matmul",
    )(x, w, w, b2, b2)[0]


def _lambda(lq1_ref, lk1_ref, lq2_ref, lk2_ref, lam_init):
    s1 = jnp.sum(lq1_ref[...] * lk1_ref[...], axis=-1, keepdims=True)
    s2 = jnp.sum(lq2_ref[...] * lk2_ref[...], axis=-1, keepdims=True)
    return jnp.exp(s1) - jnp.exp(s2) + lam_init


def _sub_layernorm(o, g_ref, lam_init):
    ms = jnp.mean(o * o, axis=-1, keepdims=True)
    return (o * lax.rsqrt(ms + EPS)) * g_ref[...] * (1.0 - lam_init)


def _attn_prompt_kernel(lq1_ref, lk1_ref, lq2_ref, lk2_ref, g_ref, q_ref, k_ref, v_ref, *rest, tq, dh, lam_init, qi):
    o_ref, s_ref, acc_ref = rest[-3:]
    scale = dh ** -0.5 * math.log2(math.e)
    q = q_ref[...]
    run_max = [None, None]
    for kj in range(qi + 1):
        k = k_ref[pl.ds(kj * tq, tq), :]
        for c in range(2):
            s = lax.dot_general(q[:, c * dh:(c + 1) * dh], k[:, c * dh:(c + 1) * dh], _NT,
                                preferred_element_type=_F32) * scale
            if kj == qi:
                row_chunk = lax.broadcasted_iota(jnp.int32, (tq, tq), 0) // CHUNK
                col_chunk = lax.broadcasted_iota(jnp.int32, (tq, tq), 1) // CHUNK
                s = jnp.where(col_chunk <= row_chunk, s, -jnp.inf)
            s_ref[pl.ds(c * tq, tq), pl.ds(kj * tq, tq)] = s
            part = s[:, :LANES]
            for j in range(1, tq // LANES):
                part = jnp.maximum(part, s[:, j * LANES:(j + 1) * LANES])
            run_max[c] = part if run_max[c] is None else jnp.maximum(run_max[c], part)
    m = jnp.concatenate(run_max, axis=0)
    row_max = jnp.broadcast_to(jnp.max(m, axis=-1, keepdims=True), m.shape)
    total = None
    for kj in range(qi + 1):
        ps = []
        for j in range(tq // LANES):
            p = jnp.exp2(s_ref[:, pl.ds(kj * tq + j * LANES, LANES)] - row_max)
            total = p if total is None else total + p
            ps.append(p.astype(_BF16))
        contrib = jnp.dot(jnp.concatenate(ps, axis=1), v_ref[pl.ds(kj * tq, tq), :], preferred_element_type=_F32)
        if kj == 0:
            acc_ref[...] = contrib
        else:
            acc_ref[...] += contrib
    lam = _lambda(lq1_ref, lk1_ref, lq2_ref, lk2_ref, lam_init)
    o = acc_ref[...] / jnp.sum(total, axis=-1, keepdims=True)
    o = o[:tq] - lam * o[tq:]
    o_ref[...] = _sub_layernorm(o, g_ref, lam_init).astype(o_ref.dtype)


def _attn_prompt(q, k, v, lam_params, subln_g, *, batch, seq, n_heads, dh, lam_init, tq=512):
    hw = 2 * dh
    assert seq % tq == 0 and tq % CHUNK == 0
    nq = seq // tq
    small = pl.BlockSpec((1, dh), lambda b, h: (0, 0))
    out = None
    for qi in range(nq):
        tile = pl.BlockSpec((tq, hw), lambda b, h, qi=qi: (b * nq + qi, h))
        in_specs = [small, small, small, small,
                    pl.BlockSpec((1, hw), lambda b, h: (0, 0)),
                    tile,
                    pl.BlockSpec((seq, hw), lambda b, h: (b, h)),
                    pl.BlockSpec((seq, hw), lambda b, h: (b, h))]
        args = [*lam_params, subln_g.reshape(1, hw), q, k, v]
        if out is not None:
            in_specs.append(pl.BlockSpec(memory_space=pl.ANY))
            args.append(out)
        out = pl.pallas_call(
            functools.partial(_attn_prompt_kernel, tq=tq, dh=dh, lam_init=lam_init, qi=qi),
            grid=(batch, n_heads),
            in_specs=in_specs,
            out_specs=tile,
            out_shape=jax.ShapeDtypeStruct(q.shape, _BF16),
            scratch_shapes=[pltpu.VMEM((2 * tq, (qi + 1) * tq), _F32), pltpu.VMEM((2 * tq, hw), _F32)],
            input_output_aliases={len(args) - 1: 0} if qi else {},
            compiler_params=_params("parallel", "parallel"),
            name="attn_prompt",
        )(*args)
    return out


def _attn_sample_kernel(lq1_ref, lk1_ref, lq2_ref, lk2_ref, g_ref, q_ref, kn_ref, vn_ref, ka_ref, kb_ref,
                        vlo_ref, vhi_ref, o_ref, *, dh, lam_init):
    scale = dh ** -0.5
    past = ka_ref.shape[0]
    hw = 2 * dh
    k_rows = [r.reshape(past * SUBLANES, dh) for r in (ka_ref, kb_ref)]
    v_rows = [r.reshape(past * SUBLANES, dh) for r in (vlo_ref, vhi_ref)]
    lam = _lambda(lq1_ref, lk1_ref, lq2_ref, lk2_ref, lam_init)
    per_block = SUBLANES // 2
    for hl in range(SUBLANES):
        cols = slice(hl * hw, (hl + 1) * hw)
        q = q_ref[:, cols]
        kn = kn_ref[:, cols]
        a_p = a_n = None
        for c in range(2):
            row = (hl % per_block) * 2 + c
            kp = k_rows[hl // per_block][pl.ds(row, past, stride=SUBLANES), :].astype(_BF16)
            qc = q[:, c * dh:(c + 1) * dh]
            s_p = lax.dot_general(qc, kp, _NT, preferred_element_type=_F32) * scale
            s_n = lax.dot_general(qc, kn[:, c * dh:(c + 1) * dh], _NT, preferred_element_type=_F32) * scale
            m = jnp.maximum(jnp.max(s_p, axis=-1, keepdims=True), jnp.max(s_n, axis=-1, keepdims=True))
            e_p = jnp.exp(s_p - m)
            e_n = jnp.exp(s_n - m)
            denom = jnp.sum(e_p, axis=-1, keepdims=True) + jnp.sum(e_n, axis=-1, keepdims=True)
            p_p = e_p / denom
            p_n = e_n / denom
            if c == 0:
                a_p, a_n = p_p, p_n
            else:
                a_p, a_n = a_p - lam * p_p, a_n - lam * p_n
        vp = jnp.concatenate([v[pl.ds(hl, past, stride=SUBLANES), :] for v in v_rows], axis=1).astype(_BF16)
        o = (jnp.dot(a_p.astype(_BF16), vp, preferred_element_type=_F32)
             + jnp.dot(a_n.astype(_BF16), vn_ref[:, cols], preferred_element_type=_F32))
        o_ref[:, cols] = _sub_layernorm(o, g_ref, lam_init).astype(o_ref.dtype)


def _attn_sample(q, k, v, cache_k, cache_v, layer, lam_params, subln_g, *, batch, seq, n_heads, dh, lam_init):
    hw = 2 * dh
    past = cache_k.shape[2]
    assert n_heads % SUBLANES == 0 and dh == LANES
    gw = SUBLANES * hw
    ck = cache_k.reshape(-1, past, n_heads * 2, dh)
    cv = cache_v.reshape(-1, past, n_heads, hw)
    base = layer * batch
    small = pl.BlockSpec((1, dh), lambda b, g: (0, 0))
    new = pl.BlockSpec((seq, gw), lambda b, g: (b, g))
    old = lambda sub, lane: pl.BlockSpec((None, past, SUBLANES, dh), lambda b, g: (base + b, 0, sub(g), lane))
    return pl.pallas_call(
        functools.partial(_attn_sample_kernel, dh=dh, lam_init=lam_init),
        grid=(batch, n_heads // SUBLANES),
        in_specs=[small, small, small, small, pl.BlockSpec((1, hw), lambda b, g: (0, 0)),
                  new, new, new,
                  old(lambda g: 2 * g, 0), old(lambda g: 2 * g + 1, 0), old(lambda g: g, 0), old(lambda g: g, 1)],
        out_specs=new,
        out_shape=jax.ShapeDtypeStruct(q.shape, _BF16),
        compiler_params=_params("parallel", "parallel"),
        name="attn_sample",
    )(*lam_params, subln_g.reshape(1, hw), q, k, v, ck, ck, cv, cv)


def _conv_ln_silu(ext_ref, c_ref, w_ref, bdw_ref, g_ref, b_ref, o_ref, *, rows):
    nslab = c_ref.shape[0]
    d = nslab * LANES
    first = CONV_HALO - (CONV_WIDTH - 1)

    def slab(si, carry):
        lanes = pl.ds(pl.multiple_of(si * LANES, LANES), LANES)
        taps = [jnp.broadcast_to(w_ref[pl.ds(j, 1), lanes], (SUBLANES, LANES)) for j in range(CONV_WIDTH)]
        bias = jnp.broadcast_to(bdw_ref[:, lanes], (SUBLANES, LANES))
        for r0 in range(0, rows, 2 * SUBLANES):
            for parity in range(2):
                acc = bias
                for j in range(CONV_WIDTH):
                    acc = acc + taps[j] * ext_ref[si, pl.ds(first + r0 + parity + j, SUBLANES, stride=2), :]
                c_ref[si, pl.ds(r0 + parity, SUBLANES, stride=2), :] = acc
        return carry

    lax.fori_loop(0, nslab, slab, 0, unroll=2)

    total = c_ref[0]
    for si in range(1, nslab):
        total = total + c_ref[si]
    mu = jnp.broadcast_to(jnp.sum(total, axis=-1, keepdims=True) * (1.0 / d), (rows, LANES))
    total = None
    for si in range(nslab):
        cc = c_ref[si] - mu
        total = cc * cc if total is None else total + cc * cc
    var = jnp.sum(total, axis=-1, keepdims=True) * (1.0 / d)
    inv = jnp.broadcast_to(lax.rsqrt(var + EPS), (rows, LANES))
    for si in range(nslab):
        lanes = slice(si * LANES, (si + 1) * LANES)
        y = (c_ref[si] - mu) * inv * g_ref[:, lanes] + b_ref[:, lanes]
        o_ref[:, lanes] = (y * jax.nn.sigmoid(y)).astype(o_ref.dtype)


def _conv_prompt_kernel(x_ref, halo_ref, w_ref, bdw_ref, g_ref, b_ref, o_ref, ext_ref, c_ref, *, rows):
    i = pl.program_id(1)
    for si in range(ext_ref.shape[0]):
        lanes = slice(si * LANES, (si + 1) * LANES)
        halo = halo_ref[:, lanes]
        ext_ref[si, pl.ds(0, CONV_HALO), :] = jnp.where(i == 0, jnp.zeros_like(halo), halo)
        ext_ref[si, pl.ds(CONV_HALO, rows), :] = x_ref[:, lanes]
    _conv_ln_silu(ext_ref, c_ref, w_ref, bdw_ref, g_ref, b_ref, o_ref, rows=rows)


def _conv_sample_kernel(x_ref, st_ref, w_ref, bdw_ref, g_ref, b_ref, o_ref, ext_ref, c_ref, *, rows):
    ctx = CONV_WIDTH - 1
    for si in range(ext_ref.shape[0]):
        lanes = slice(si * LANES, (si + 1) * LANES)
        ext_ref[si, pl.ds(CONV_HALO - ctx, ctx), :] = st_ref[0, :, lanes]
        ext_ref[si, pl.ds(CONV_HALO, rows), :] = x_ref[:, lanes]
    _conv_ln_silu(ext_ref, c_ref, w_ref, bdw_ref, g_ref, b_ref, o_ref, rows=rows)


def _conv_module(x, state, w_dw, b_dw, ln_g, ln_b, *, batch, seq):
    d = x.shape[1]
    vec = lambda a: a.reshape(1, d)
    if state is None:
        rows = 128
        nt = seq // rows
        grid = (batch, nt)
        per = rows // CONV_HALO
        const = lambda b, i: (0, 0)
        in_specs = [pl.BlockSpec((rows, d), lambda b, i: (b * nt + i, 0)),
                    pl.BlockSpec((CONV_HALO, d), lambda b, i: (jnp.maximum((b * nt + i) * per - 1, 0), 0))]
        out_spec = pl.BlockSpec((rows, d), lambda b, i: (b * nt + i, 0))
        body = functools.partial(_conv_prompt_kernel, rows=rows)
        args = [x, x]
        sem = ("parallel", "arbitrary")
        name = "conv_prompt"
    else:
        rows = seq
        grid = (batch,)
        const = lambda b: (0, 0)
        in_specs = [pl.BlockSpec((rows, d), lambda b: (b, 0)),
                    pl.BlockSpec((1, CONV_WIDTH - 1, d), lambda b: (b, 0, 0))]
        out_spec = pl.BlockSpec((rows, d), lambda b: (b, 0))
        body = functools.partial(_conv_sample_kernel, rows=rows)
        args = [x, state]
        sem = ("parallel",)
        name = "conv_sample"
    in_specs += [pl.BlockSpec((CONV_WIDTH, d), const)] + [pl.BlockSpec((1, d), const)] * 3
    return pl.pallas_call(
        body,
        grid=grid,
        in_specs=in_specs,
        out_specs=out_spec,
        out_shape=jax.ShapeDtypeStruct(x.shape, _BF16),
        scratch_shapes=[pltpu.VMEM((d // LANES, CONV_HALO + rows, LANES), _F32),
                        pltpu.VMEM((d // LANES, rows, LANES), _F32)],
        compiler_params=_params(*sem),
        name=name,
    )(*args, w_dw, vec(b_dw), vec(ln_g), vec(ln_b))


def _top_rows(s, k):
    n = s.shape[0]
    iota = lax.broadcasted_iota(jnp.int32, s.shape, 0).astype(_F32)
    vals, idxs = [], []
    for _ in range(k):
        m = jnp.max(s, axis=0, keepdims=True)
        am = jnp.min(jnp.where(s == m, iota, float(n)), axis=0, keepdims=True)
        vals.append(m)
        idxs.append(am)
        s = jnp.where(iota == am, -jnp.inf, s)
    return jnp.concatenate(vals, axis=0), jnp.concatenate(idxs, axis=0)


def _pick_row(rank, table):
    out = jnp.zeros_like(rank)
    for k in range(table.shape[0]):
        out = jnp.where(rank == float(k), table[k:k + 1], out)
    return out


def _router_kernel(q_ref, keys_ref, a_ref, b_ref, g_ref, *, half):
    kk = PEER_TOPK
    lim = 4
    assert lim * lim >= kk
    q = q_ref[...]
    tops = []
    for c in range(2):
        s = lax.dot_general(keys_ref[0, c], q[:, c * half:(c + 1) * half], _NT, preferred_element_type=_F32)
        tops.append(_top_rows(s, kk))
    (s0, i0), (s1, i1) = tops
    rank = lax.broadcasted_iota(jnp.int32, s0.shape, 0)
    rank_f = rank.astype(_F32)
    never = float(kk * kk)
    cands, flats = [], []
    for k1 in range(lim):
        ok = rank < kk // (k1 + 1)
        cands.append(jnp.where(ok, s0[k1:k1 + 1] + s1, -jnp.inf))
        flats.append(jnp.where(ok, k1 * kk + rank_f, never))
    for k2 in range(lim):
        if kk // (k2 + 1) <= lim:
            continue
        ok = (rank >= lim) & (rank < kk // (k2 + 1))
        cands.append(jnp.where(ok, s0 + s1[k2:k2 + 1], -jnp.inf))
        flats.append(jnp.where(ok, rank_f * kk + k2, never))
    cand = jnp.concatenate(cands, axis=0)
    flat = jnp.concatenate(flats, axis=0)
    best, pos = [], []
    for _ in range(kk):
        m = jnp.max(cand, axis=0, keepdims=True)
        p = jnp.min(jnp.where(cand == m, flat, never), axis=0, keepdims=True)
        best.append(m)
        pos.append(p)
        cand = jnp.where(flat == p, -jnp.inf, cand)
    best = jnp.concatenate(best, axis=0)
    pos = jnp.concatenate(pos, axis=0)
    k1 = jnp.floor(pos * (1.0 / kk))
    k2 = pos - k1 * kk
    e = jnp.exp(best - best[0:1])
    g_ref[...] = e / jnp.sum(e, axis=0, keepdims=True)
    a_ref[...] = _pick_row(k1, i0).astype(jnp.int32)
    b_ref[...] = _pick_row(k2, i1).astype(jnp.int32)


def _peer_route(q, sub_keys, *, tm=1024):
    n = q.shape[0]
    tm = min(tm, n)
    heads, _, n_keys, half = sub_keys.shape
    nsel = heads * PEER_TOPK
    out_spec = pl.BlockSpec((PEER_TOPK, tm), lambda i, h: (h, i))
    return pl.pallas_call(
        functools.partial(_router_kernel, half=half),
        grid=(n // tm, heads),
        in_specs=[pl.BlockSpec((tm, 2 * half), lambda i, h: (i, h)),
                  pl.BlockSpec((1, 2, n_keys, half), lambda i, h: (h, 0, 0, 0))],
        out_specs=[out_spec, out_spec, out_spec],
        out_shape=[jax.ShapeDtypeStruct((nsel, n), jnp.int32), jax.ShapeDtypeStruct((nsel, n), jnp.int32),
                   jax.ShapeDtypeStruct((nsel, n), _F32)],
        compiler_params=_params("parallel", "parallel"),
        name="peer_route",
    )(q, sub_keys)


def _gate_matrix_kernel(a_ref, b_ref, g_ref, o_ref, at_ref, bt_ref, gt_ref, s_ref, *, n_keys, tg):
    at_ref[...] = a_ref[...].T
    bt_ref[...] = b_ref[...].T
    gt_ref[...] = g_ref[...].T
    nsel = at_ref.shape[1]
    pitch = _gate_pitch(n_keys)
    half = SUBLANES // 2
    key_id = lax.broadcasted_iota(jnp.int32, (n_keys, nsel), 0)
    key_id_odd = lax.broadcasted_iota(jnp.int32, (n_keys + SUBLANES, nsel), 0) - half

    def grid_of(t, ids):
        row = pl.ds(t, 1)
        lhs = jnp.where(at_ref[row, :] == ids, gt_ref[row, :], 0.0).astype(_BF16)
        rhs = jnp.where(bt_ref[row, :] == key_id, 1.0, 0.0).astype(_BF16)
        return lax.dot_general(lhs, rhs, _NT, preferred_element_type=_F32)

    def token_pair(p, carry):
        base = pl.multiple_of(p * (2 * pitch), SUBLANES)
        s_ref[pl.ds(base, n_keys), :] = grid_of(2 * p, key_id)
        odd = pl.multiple_of(base + pitch - half, SUBLANES)
        s_ref[pl.ds(odd, n_keys + SUBLANES), :] = grid_of(2 * p + 1, key_id_odd)
        return carry

    lax.fori_loop(0, tg // 2, token_pair, 0, unroll=16)
    for i in range(n_keys):
        o_ref[:, i * n_keys:(i + 1) * n_keys] = s_ref[pl.ds(i, tg, stride=pitch), :].astype(o_ref.dtype)


def _gate_pitch(n_keys):
    assert n_keys % SUBLANES == 0
    return n_keys + SUBLANES // 2


def _gate_matrix(sel_a, sel_b, gate, *, n_keys, tg=128):
    nsel, n = sel_a.shape
    in_spec = pl.BlockSpec((nsel, tg), lambda i: (0, i))
    return pl.pallas_call(
        functools.partial(_gate_matrix_kernel, n_keys=n_keys, tg=tg),
        grid=(n // tg,),
        in_specs=[in_spec, in_spec, in_spec],
        out_specs=pl.BlockSpec((tg, n_keys * n_keys), lambda i: (i, 0)),
        out_shape=jax.ShapeDtypeStruct((n, n_keys * n_keys), _BF16),
        scratch_shapes=[pltpu.VMEM((tg, nsel), jnp.int32), pltpu.VMEM((tg, nsel), jnp.int32),
                        pltpu.VMEM((tg, nsel), _F32), pltpu.VMEM((tg * _gate_pitch(n_keys), n_keys), _F32)],
        compiler_params=_params("parallel"),
        name="peer_gates",
    )(sel_a, sel_b, gate)


def _peer_weights_kernel(x_ref, u_ref, gates_ref, *rest):
    n_side = (len(rest) - 1) // 2
    o_ref = rest[n_side]
    act = lax.dot_general(x_ref[...], u_ref[...].astype(x_ref.dtype), _NT, preferred_element_type=_F32)
    gelu = 0.5 * act * (1.0 + lax.erf(act * (2.0 ** -0.5)))
    o_ref[...] = (gelu * gates_ref[...].astype(_F32)).astype(o_ref.dtype)
    for src, dst in zip(rest[:n_side], rest[n_side + 1:]):
        dst[...] = src[...].astype(dst.dtype)


def _peer_weights(x, u, layer, gates, side=(), *, tm=1024, te=512):
    n, d = x.shape
    n_exp = u.shape[1]
    tm = min(tm, n)
    grid = (n // tm, n_exp // te)
    side_in, side_out, side_shape = _side_specs(side, grid)
    outs = _streamed_call(
        _peer_weights_kernel,
        grid=grid,
        in_specs=[pl.BlockSpec((tm, d), lambda i, j: (i, 0)),
                  pl.BlockSpec((None, te, d), lambda i, j: (layer, j, 0), pipeline_mode=WEIGHT_BUFFERS),
                  pl.BlockSpec((tm, te), lambda i, j: (i, j))] + side_in,
        out_specs=[pl.BlockSpec((tm, te), lambda i, j: (i, j))] + side_out,
        out_shape=[jax.ShapeDtypeStruct((n, n_exp), _BF16)] + side_shape,
        name="peer_weights",
    )(x, u, gates, *[s[0] for s in side])
    return outs[0], list(outs[1:])


def _peer_route_gates(h, norm_g, w_q, sub_keys):
    assert h.shape[0] % LANES == 0
    hn = _rmsnorm(h, norm_g, _BF16)
    q = _matmul(hn, w_q, out_dtypes=(_BF16,), name="peer_query")
    sel_a, sel_b, gate = _peer_route(q, sub_keys)
    return hn, _gate_matrix(sel_a, sel_b, gate, n_keys=sub_keys.shape[2])


def _peer_mix(weights, v_tab, h):
    return _matmul(weights, v_tab, resid=h, tiles=(1024, 1024, 2048), name="peer_mix")


def kernel(x_prompt, x_sample, cache_k, cache_v, state_conv, mixer_norm_g, ffn_norm_g, final_norm_g, w_qkv, lambda_q1, lambda_k1, lambda_q2, lambda_k2, subln_g, w_o, w_pw1, b_pw1, w_dw, b_dw, conv_ln_g, conv_ln_b, w_pw2, b_pw2, peer_wq, peer_sub_keys, peer_u, peer_v):
    batch, seq, d = x_prompt.shape
    dbatch, dseq, _ = x_sample.shape
    depth = mixer_norm_g.shape[0]
    n_heads, dh = cache_k.shape[3], cache_k.shape[5]
    past = cache_k.shape[2]
    assert dseq == CHUNK and past % CHUNK == 0
    sets = [dict(h=x_prompt.reshape(batch * seq, d), batch=batch, seq=seq, prompt=True),
            dict(h=x_sample.reshape(dbatch * dseq, d), batch=dbatch, seq=dseq, prompt=False)]
    new_k, new_v, new_conv = [[], []], [[], []], [[], []]
    early = {}

    for i in range(depth):
        if i % 2 == 0:
            a = i // 2
            lam_init = 0.8 - 0.6 * math.exp(-0.3 * i)
            wq = _layer_bf16(w_qkv, a, 0, d)
            wk = wv = wo = None
            lam_params = [p[a].reshape(1, dh) for p in (lambda_q1, lambda_k1, lambda_q2, lambda_k2)]
            for si, st in enumerate(sets):
                first = st["prompt"]
                hn = _rmsnorm(st["h"], mixer_norm_g[i], _BF16)
                q = _matmul(hn, wq, out_dtypes=(_BF16,), side=[(w_qkv, a, d, d)] if first else (), name="q_proj")
                if first:
                    q, (wk,) = q
                kk = _matmul(hn, wk, out_dtypes=(_F32, _BF16), lane_rows_first=True,
                             tiles=(512, SUBLANES * LANES, d), side=[(w_qkv, a, 2 * d, d)] if first else (),
                             name="k_proj")
                if first:
                    kk, (wv,) = kk
                k32, k16 = kk
                vv = _matmul(hn, wv, out_dtypes=(_F32, _BF16), side=[(w_o, a, 0, d)] if first else (), name="v_proj")
                if first:
                    vv, (wo,) = vv
                v32, v16 = vv
                kw = dict(batch=st["batch"], seq=st["seq"], n_heads=n_heads, dh=dh, lam_init=lam_init)
                if st["prompt"]:
                    o = _attn_prompt(q, k16, v16, lam_params, subln_g[a], **kw)
                else:
                    o = _attn_sample(q, k16, v16, cache_k, cache_v, a, lam_params, subln_g[a], **kw)
                h_new = _matmul(o, wo, resid=st["h"], side=[(peer_wq, i, 0, peer_wq.shape[2])] if first else (),
                                name="o_proj")
                if first:
                    h_new, (early[("peer_wq", i)],) = h_new
                st["h"] = h_new
                new_k[si].append(k32.reshape(st["batch"], st["seq"], n_heads, 2, dh))
                new_v[si].append(v32.reshape(st["batch"], st["seq"], n_heads, 2 * dh))
        else:
            c = i // 2
            w1, w2 = early.pop(("w_pw1", c), None), early.pop(("w_pw2", c), None)
            if w1 is None:
                w1, w2 = _layer_bf16(w_pw1, c), _layer_bf16(w_pw2, c)
            for si, st in enumerate(sets):
                hn = _rmsnorm(st["h"], mixer_norm_g[i], _BF16)
                glu = _glu_matmul(hn, w1, b_pw1[c])
                state = None if st["prompt"] else state_conv[c]
                z = _conv_module(glu, state, w_dw[c], b_dw[c], conv_ln_g[c], conv_ln_b[c],
                                 batch=st["batch"], seq=st["seq"])
                st["h"] = _matmul(z, w2, bias=b_pw2[c], resid=st["h"], name="pw2")
                glu3 = glu.reshape(st["batch"], st["seq"], d)
                if st["prompt"]:
                    new_conv[si].append(glu3[:, seq - (CONV_WIDTH - 1):])
                else:
                    ctx = jnp.concatenate([state, glu3], axis=1)
                    new_conv[si].append(ctx[:, -(CONV_WIDTH - 1):])
        wpq = early.pop(("peer_wq", i), None)
        if wpq is None:
            wpq = _layer_bf16(peer_wq, i)
        keys = peer_sub_keys[i].astype(_BF16)
        jobs = [("peer_v", peer_v, i)]
        if i + 1 < depth and (i + 1) % 2 == 1:
            jobs += [("peer_wq", peer_wq, i + 1), ("w_pw1", w_pw1, (i + 1) // 2), ("w_pw2", w_pw2, (i + 1) // 2)]
        v_tab = None
        for st in sets:
            hn, gates = _peer_route_gates(st["h"], ffn_norm_g[i], wpq, keys)
            side = [(w, wl, 0, w.shape[2]) for _, w, wl in jobs] if st["prompt"] else []
            weights, copies = _peer_weights(hn, peer_u, i, gates, side)
            if st["prompt"]:
                v_tab = copies[0]
                early.update({(name, wl): cp for (name, _, wl), cp in zip(jobs[1:], copies[1:])})
            st["h"] = _peer_mix(weights, v_tab, st["h"])

    y_prompt = _rmsnorm(sets[0]["h"], final_norm_g, _F32).reshape(batch, seq, d)
    y_sample = _rmsnorm(sets[1]["h"], final_norm_g, _F32).reshape(dbatch, dseq, d)
    return (y_prompt, y_sample, jnp.stack(new_k[0]), jnp.stack(new_v[0]), jnp.stack(new_conv[0]),
            jnp.stack(new_k[1]), jnp.stack(new_v[1]), jnp.stack(new_conv[1]))
```
